```python
import math
import jax, jax.numpy as jnp
from jax import lax
import numpy as np

D_MODEL = 2048
BATCH = 1
SEQ = 8192
DEPTH = 2
DEC_BATCH = 32
DEC_SEQ = 4
PAST_LEN = 8192
PAGE_SIZE = 128

HEAD_DIM = 128
N_MIX_HEADS = D_MODEL // HEAD_DIM
N_ATT_HEADS = (3 * N_MIX_HEADS) // 8
N_GDN_HEADS = (3 * N_MIX_HEADS) // 8
ATT_W = N_ATT_HEADS * HEAD_DIM
GDN_DK = HEAD_DIM
GDN_DV = HEAD_DIM
GDN_W = N_GDN_HEADS * GDN_DV
POOL_W = D_MODEL - ATT_W - GDN_W
POOL_WINDOWS = (2, 4, 8, 16)
POOL_GROUPS = 4
POOL_GROUP_W = POOL_W // POOL_GROUPS
POOL_BUF = 15
MOBA_BLOCK = 256
MOBA_TOPK = 3
Q_BLOCK = 128
GDN_CHUNK = 64
GDN_CONV = 4
D_FF = (11 * D_MODEL) // 4
FFN_CONV = 3
ROPE_THETA = 10000.0
EPS = 1e-6
IN_SIZES = (ATT_W, ATT_W, ATT_W, 3 * GDN_W, GDN_W, N_GDN_HEADS, N_GDN_HEADS, POOL_W)
IN_W = sum(IN_SIZES)

kernel_name = 'moba_gdn_pool_hybrid_step'


def rms_norm(x, g):
    xf = x.astype(jnp.float32)
    y = xf * lax.rsqrt(jnp.mean(xf * xf, axis=-1, keepdims=True) + EPS)
    return (y * g.astype(jnp.float32)).astype(x.dtype)


def rope(x, pos):
    hd = x.shape[-1]
    half = hd // 2
    inv = ROPE_THETA ** (-jnp.arange(half, dtype=jnp.float32) * 2.0 / hd)
    ang = pos.astype(jnp.float32)[:, None] * inv[None, :]
    cos = jnp.cos(ang)[None, :, None, :]
    sin = jnp.sin(ang)[None, :, None, :]
    xf = x.astype(jnp.float32)
    x1, x2 = xf[..., :half], xf[..., half:]
    return jnp.concatenate([x1 * cos - x2 * sin, x2 * cos + x1 * sin], axis=-1).astype(x.dtype)


def causal_conv(x_ext, w):
    return lax.conv_general_dilated(
        x_ext, w[:, None, :].astype(x_ext.dtype), (1,), 'VALID',
        dimension_numbers=('NWC', 'WIO', 'NWC'), feature_group_count=x_ext.shape[-1])


def l2norm(x):
    xf = x.astype(jnp.float32)
    return xf * lax.rsqrt(jnp.sum(xf * xf, axis=-1, keepdims=True) + EPS)


def gated_rms_norm(o, z, w):
    o = o * lax.rsqrt(jnp.mean(o * o, axis=-1, keepdims=True) + EPS) * w.astype(jnp.float32)
    return (o * jax.nn.silu(z.astype(jnp.float32))).astype(z.dtype)


def pad_to_blocks(parts):
    total = sum(p.shape[1] for p in parts)
    pad = (-total) % MOBA_BLOCK
    if pad:
        p0 = parts[-1]
        parts = parts + [jnp.zeros((p0.shape[0], pad) + p0.shape[2:], p0.dtype)]
    return jnp.concatenate(parts, axis=1) if len(parts) > 1 else parts[0]


def moba_attention(q, k, v, start):
    f32 = jnp.float32
    B, Tq, H, HD = q.shape
    n_blk = k.shape[1] // MOBA_BLOCK
    kb = k.reshape(B, n_blk, MOBA_BLOCK, H, HD)
    vb = v.reshape(B, n_blk, MOBA_BLOCK, H, HD)
    k_mean = jnp.mean(kb, axis=2, dtype=f32)
    top = min(MOBA_TOPK, n_blk)
    chunk = min(Q_BLOCK, Tq)
    n_q = -(-Tq // chunk)
    if n_q * chunk != Tq:
        q = jnp.pad(q, ((0, 0), (0, n_q * chunk - Tq), (0, 0), (0, 0)))
    scale = HD ** -0.5
    b_ix = jnp.arange(B)[:, None, None, None]
    h_ix = jnp.arange(H)[None, None, :, None]
    blk = jnp.arange(n_blk)
    slot = jnp.arange(top)
    key_off = jnp.arange(MOBA_BLOCK)
    n_sel = top * MOBA_BLOCK

    def query_block(ci):
        q0 = ci * chunk
        qc = lax.dynamic_slice_in_dim(q, q0, chunk, axis=1)
        q_pos = start + q0 + jnp.arange(chunk)
        own = (q_pos // MOBA_BLOCK)[None, :, None, None]
        gate = jnp.einsum('bchd,bnhd->bchn', qc.astype(f32), k_mean)
        gate = jnp.where(blk < own, gate, -jnp.inf)
        _, sel = lax.top_k(gate, top)
        k_sel = kb[b_ix, sel, :, h_ix]
        v_sel = vb[b_ix, sel, :, h_ix]
        s_sel = jnp.einsum('bchd,bchkjd->bchkj', qc, k_sel, preferred_element_type=f32) * scale
        s_sel = jnp.where((slot < own)[..., None], s_sel, -jnp.inf)
        own0 = (start + q0) // MOBA_BLOCK
        k_own = lax.dynamic_index_in_dim(kb, own0, axis=1, keepdims=False)
        v_own = lax.dynamic_index_in_dim(vb, own0, axis=1, keepdims=False)
        s_own = jnp.einsum('bchd,bjhd->bchj', qc, k_own, preferred_element_type=f32) * scale
        key_pos = own0 * MOBA_BLOCK + key_off
        s_own = jnp.where(key_pos[None, None, None, :] <= q_pos[None, :, None, None], s_own, -jnp.inf)
        p = jax.nn.softmax(jnp.concatenate([s_sel.reshape(B, chunk, H, n_sel), s_own], axis=-1), axis=-1)
        p_sel = p[..., :n_sel].reshape(B, chunk, H, top, MOBA_BLOCK).astype(v.dtype)
        p_own = p[..., n_sel:].astype(v.dtype)
        o = (jnp.einsum('bchkj,bchkjd->bchd', p_sel, v_sel, preferred_element_type=f32)
             + jnp.einsum('bchj,bjhd->bchd', p_own, v_own, preferred_element_type=f32))
        return o.astype(q.dtype)

    out = lax.map(query_block, jnp.arange(n_q))
    return jnp.moveaxis(out, 0, 1).reshape(B, n_q * chunk, H, HD)[:, :Tq]


def gated_delta_rule(q, k, v, g, beta, s0):
    f32 = jnp.float32
    B, T, H, DK = q.shape
    DV = v.shape[-1]
    C = min(GDN_CHUNK, T)
    n = -(-T // C)
    pad = n * C - T

    def chunks(a):
        if pad:
            a = jnp.pad(a, [(0, 0), (0, pad)] + [(0, 0)] * (a.ndim - 2))
        a = a.reshape((B, n, C) + a.shape[2:])
        return jnp.moveaxis(a, (1, 2), (0, 3))

    qc, kc, vc, gc, bc = [chunks(a) for a in (q, k, v, g, beta)]
    g_cum = jnp.cumsum(gc, axis=-1)
    idx = jnp.arange(C)
    causal = idx[:, None] >= idx[None, :]
    strict = idx[:, None] > idx[None, :]
    decay = jnp.exp(jnp.where(causal, g_cum[..., :, None] - g_cum[..., None, :], -jnp.inf))
    kk = jnp.einsum('nbhid,nbhjd->nbhij', kc, kc)
    a_mat = jnp.where(strict, bc[..., :, None] * kk * decay, 0.0) + jnp.eye(C, dtype=f32)
    rhs = jnp.concatenate([vc * bc[..., None], kc * (bc * jnp.exp(g_cum))[..., None]], axis=-1)
    sol = lax.linalg.triangular_solve(a_mat, rhs, left_side=True, lower=True, unit_diagonal=True)
    u, w = sol[..., :DV], sol[..., DV:]
    qk = jnp.where(causal, jnp.einsum('nbhid,nbhjd->nbhij', qc, kc) * decay, 0.0)

    def step(S, xs):
        q_i, k_i, u_i, w_i, g_i, qk_i = xs
        v_new = u_i - jnp.einsum('bhck,bhkv->bhcv', w_i, S)
        o = (jnp.einsum('bhck,bhkv->bhcv', q_i * jnp.exp(g_i)[..., None], S)
             + jnp.einsum('bhij,bhjv->bhiv', qk_i, v_new))
        g_last = g_i[..., -1:]
        S = (S * jnp.exp(g_last)[..., None]
             + jnp.einsum('bhck,bhcv->bhkv', k_i * jnp.exp(g_last - g_i)[..., None], v_new))
        return S, o

    S, o = lax.scan(step, s0, (qc, kc, u, w, g_cum, qk))
    o = jnp.moveaxis(o, (0, 3), (1, 2)).reshape(B, n * C, H, DV)[:, :T]
    return o, S


def pool_mix(p, buf, pos, w, scale):
    B, T, _ = p.shape
    ext = jnp.concatenate([buf, p], axis=1)
    new_buf = ext[:, -POOL_BUF:]
    cs = jnp.pad(jnp.cumsum(ext.astype(jnp.float32), axis=1), ((0, 0), (1, 0), (0, 0)))
    end = cs[:, POOL_BUF + 1:]
    means = []
    for gi, win in enumerate(POOL_WINDOWS):
        lo, hi = gi * POOL_GROUP_W, (gi + 1) * POOL_GROUP_W
        s = end[..., lo:hi] - cs[:, POOL_BUF + 1 - win:POOL_BUF + 1 - win + T, lo:hi]
        cnt = jnp.minimum(win, pos + 1).astype(jnp.float32)[None, :, None]
        means.append(s / cnt)
    mix = (jnp.concatenate(means, axis=-1) - p.astype(jnp.float32)).reshape(B, T, POOL_GROUPS, POOL_GROUP_W)
    out = jnp.einsum('btgc,gcd->btgd', mix, w.astype(jnp.float32)).reshape(B, T, POOL_W)
    return (out * scale.astype(jnp.float32)).astype(p.dtype), new_buf


def trunk_layer(x, c, start, k_past, v_past, s_gdn, gdn_buf, pool_buf, ffn_buf, lw):
    (w_ada, b_ada, g_mix, g_ffn, w_in, gdn_conv_w, gdn_a_log, gdn_dt_bias, gdn_norm_w,
     pool_w, pool_scale, w_out, w_gate, w_up, ffn_conv_w, w_down) = lw
    f32 = jnp.float32
    B, T, _ = x.shape
    pos = start + jnp.arange(T, dtype=jnp.int32)
    mod = jnp.einsum('bd,de->be', jax.nn.silu(c), w_ada) + b_ada
    sh1, sc1, gt1, sh2, sc2, gt2 = [m[:, None, :] for m in jnp.split(mod, 6, axis=-1)]

    h = rms_norm(x, g_mix) * (1 + sc1) + sh1
    proj = jnp.einsum('btd,de->bte', h, w_in)
    split_at = [int(s) for s in np.cumsum(IN_SIZES)[:-1]]
    qa, ka, va, qkv_b, z_b, a_b, b_b, p_in = jnp.split(proj, split_at, axis=-1)

    qa = rope(qa.reshape(B, T, N_ATT_HEADS, HEAD_DIM), pos)
    ka = rope(ka.reshape(B, T, N_ATT_HEADS, HEAD_DIM), pos)
    va = va.reshape(B, T, N_ATT_HEADS, HEAD_DIM)
    k_all = pad_to_blocks([ka] if k_past is None else [k_past, ka])
    v_all = pad_to_blocks([va] if v_past is None else [v_past, va])
    o_a = moba_attention(qa, k_all, v_all, start).reshape(B, T, ATT_W)

    ext = jnp.concatenate([gdn_buf, qkv_b], axis=1)
    new_gdn_buf = ext[:, -(GDN_CONV - 1):]
    qb, kb, vb = jnp.split(jax.nn.silu(causal_conv(ext, gdn_conv_w)), 3, axis=-1)
    qb = l2norm(qb.reshape(B, T, N_GDN_HEADS, GDN_DK)) * (GDN_DK ** -0.5)
    kb = l2norm(kb.reshape(B, T, N_GDN_HEADS, GDN_DK))
    vb = vb.reshape(B, T, N_GDN_HEADS, GDN_DV).astype(f32)
    beta = jax.nn.sigmoid(b_b.astype(f32))
    g = -jnp.exp(gdn_a_log.astype(f32)) * jax.nn.softplus(a_b.astype(f32) + gdn_dt_bias.astype(f32))
    o_b, s_new = gated_delta_rule(qb, kb, vb, g, beta, s_gdn.astype(f32))
    o_b = gated_rms_norm(o_b, z_b.reshape(B, T, N_GDN_HEADS, GDN_DV), gdn_norm_w).reshape(B, T, GDN_W)

    o_p, new_pool_buf = pool_mix(p_in, pool_buf, pos, pool_w, pool_scale)

    mixed = jnp.einsum('bte,ed->btd', jnp.concatenate([o_a, o_b, o_p], axis=-1), w_out)
    x = x + gt1 * mixed

    h = rms_norm(x, g_ffn) * (1 + sc2) + sh2
    a = jnp.einsum('btd,df->btf', h, w_gate)
    u = jnp.einsum('btd,df->btf', h, w_up)
    a_ext = jnp.concatenate([ffn_buf, a], axis=1)
    new_ffn_buf = a_ext[:, -(FFN_CONV - 1):]
    act = jax.nn.silu(causal_conv(a_ext, ffn_conv_w)) * u
    x = x + gt2 * jnp.einsum('btf,fd->btd', act, w_down)
    return x, ka, va, s_new.astype(x.dtype), new_gdn_buf, new_pool_buf, new_ffn_buf


def setup_inputs(seed: int = 0) -> dict:
    key = jax.random.key(seed)
    ks = jax.random.split(key, 32)
    f32 = jnp.float32
    n_pages = PAST_LEN // PAGE_SIZE
    n_used = DEC_BATCH * n_pages
    n_phys = n_used + n_used // 4

    def nrm(k, shape, s=1.0):
        return jax.random.normal(k, shape, f32) * s

    kv_shape = (DEPTH, n_phys, PAGE_SIZE, N_ATT_HEADS, HEAD_DIM)
    page_table = jax.random.permutation(ks[4], n_phys)[:n_used].reshape(DEC_BATCH, n_pages).astype(jnp.int32)
    dt = jnp.exp(jax.random.uniform(ks[15], (DEPTH, N_GDN_HEADS), f32, math.log(1e-3), math.log(1e-1)))
    return {
        'x_prompt': nrm(ks[0], (BATCH, SEQ, D_MODEL)),
        'x_sample': nrm(ks[1], (DEC_BATCH, DEC_SEQ, D_MODEL)),
        'cache_k': nrm(ks[2], kv_shape),
        'cache_v': nrm(ks[3], kv_shape),
        'page_table': page_table,
        'state_gdn': nrm(ks[5], (DEPTH, DEC_BATCH, N_GDN_HEADS, GDN_DK, GDN_DV), 0.5),
        'state_gdn_conv': nrm(ks[6], (DEPTH, DEC_BATCH, GDN_CONV - 1, 3 * GDN_W)),
        'state_pool': nrm(ks[7], (DEPTH, DEC_BATCH, POOL_BUF, POOL_W)),
        'state_ffn_conv': nrm(ks[8], (DEPTH, DEC_BATCH, FFN_CONV - 1, D_FF)),
        'c_prompt': nrm(ks[9], (BATCH, D_MODEL)),
        'c_sample': nrm(ks[10], (DEC_BATCH, D_MODEL)),
        'w_ada': nrm(ks[11], (DEPTH, D_MODEL, 6 * D_MODEL), D_MODEL ** -0.5),
        'b_ada': nrm(ks[12], (DEPTH, 6 * D_MODEL), 0.02),
        'g_mix': 1.0 + nrm(ks[13], (DEPTH, D_MODEL), 0.02),
        'g_ffn': 1.0 + nrm(ks[14], (DEPTH, D_MODEL), 0.02),
        'w_in': nrm(ks[16], (DEPTH, D_MODEL, IN_W), D_MODEL ** -0.5),
        'gdn_conv_w': nrm(ks[17], (DEPTH, GDN_CONV, 3 * GDN_W), GDN_CONV ** -0.5),
        'gdn_a_log': jnp.log(jax.random.uniform(ks[18], (DEPTH, N_GDN_HEADS), f32, 1.0, 16.0)),
        'gdn_dt_bias': jnp.log(jnp.expm1(dt)),
        'gdn_norm_w': 1.0 + nrm(ks[19], (DEPTH, GDN_DV), 0.02),
        'pool_w': nrm(ks[20], (DEPTH, POOL_GROUPS, POOL_GROUP_W, POOL_GROUP_W), POOL_GROUP_W ** -0.5),
        'pool_scale': 1.0 + nrm(ks[21], (DEPTH, POOL_W), 0.1),
        'w_out': nrm(ks[22], (DEPTH, D_MODEL, D_MODEL), D_MODEL ** -0.5),
        'w_gate': nrm(ks[23], (DEPTH, D_MODEL, D_FF), D_MODEL ** -0.5),
        'w_up': nrm(ks[24], (DEPTH, D_MODEL, D_FF), D_MODEL ** -0.5),
        'ffn_conv_w': nrm(ks[25], (DEPTH, FFN_CONV, D_FF), FFN_CONV ** -0.5),
        'w_down': nrm(ks[26], (DEPTH, D_FF, D_MODEL), D_FF ** -0.5),
        'g_final': 1.0 + nrm(ks[27], (D_MODEL,), 0.02),
    }


def reference(x_prompt, x_sample, cache_k, cache_v, page_table, state_gdn, state_gdn_conv, state_pool,
              state_ffn_conv, c_prompt, c_sample, w_ada, b_ada, g_mix, g_ffn, w_in, gdn_conv_w, gdn_a_log,
              gdn_dt_bias, gdn_norm_w, pool_w, pool_scale, w_out, w_gate, w_up, ffn_conv_w, w_down, g_final):
    B = x_prompt.shape[0]
    DB = x_sample.shape[0]
    past_len = page_table.shape[1] * cache_k.shape[2]
    dt = x_prompt.dtype
    s0 = jnp.zeros((B, N_GDN_HEADS, GDN_DK, GDN_DV), dt)
    gb0 = jnp.zeros((B, GDN_CONV - 1, 3 * GDN_W), dt)
    pb0 = jnp.zeros((B, POOL_BUF, POOL_W), dt)
    fb0 = jnp.zeros((B, FFN_CONV - 1, D_FF), dt)
    xp, xs = x_prompt, x_sample
    new_p, new_s = [], []
    for l in range(DEPTH):
        lw = (w_ada[l], b_ada[l], g_mix[l], g_ffn[l], w_in[l], gdn_conv_w[l], gdn_a_log[l], gdn_dt_bias[l],
              gdn_norm_w[l], pool_w[l], pool_scale[l], w_out[l], w_gate[l], w_up[l], ffn_conv_w[l], w_down[l])
        xp, *st_p = trunk_layer(xp, c_prompt, 0, None, None, s0, gb0, pb0, fb0, lw)
        k_past = cache_k[l, page_table].reshape(DB, past_len, N_ATT_HEADS, HEAD_DIM)
        v_past = cache_v[l, page_table].reshape(DB, past_len, N_ATT_HEADS, HEAD_DIM)
        xs, *st_s = trunk_layer(xs, c_sample, past_len, k_past, v_past, state_gdn[l], state_gdn_conv[l],
                                state_pool[l], state_ffn_conv[l], lw)
        new_p.append(st_p)
        new_s.append(st_s)
    k_prompt, v_prompt, gdn_prompt, gdn_conv_prompt, pool_prompt, ffn_conv_prompt = [jnp.stack(t) for t in zip(*new_p)]
    k_sample, v_sample, gdn_sample, gdn_conv_sample, pool_sample, ffn_conv_sample = [jnp.stack(t) for t in zip(*new_s)]
    y_prompt = rms_norm(xp, g_final)
    y_sample = rms_norm(xs, g_final)
    return (y_prompt, y_sample, k_prompt, v_prompt, k_sample, v_sample, gdn_prompt, gdn_sample,
            gdn_conv_prompt, gdn_conv_sample, pool_prompt, pool_sample, ffn_conv_prompt, ffn_conv_sample)
```

```python
import functools

import jax
import jax.numpy as jnp
from jax import lax
from jax.experimental import pallas as pl
from jax.experimental.pallas import tpu as pltpu

f32 = jnp.float32
bf16 = jnp.bfloat16
HIGHEST = lax.Precision.HIGHEST

LANES = 128
SUBLANES = 8
VMEM_LIMIT_BYTES = 56 * 1024 * 1024

HEAD_DIM = 128
N_ATT_HEADS = 6
N_GDN_HEADS = 6
ATT_W = N_ATT_HEADS * HEAD_DIM
GDN_W = N_GDN_HEADS * HEAD_DIM
POOL_WINDOWS = (2, 4, 8, 16)
POOL_GROUP_W = 128
POOL_W = len(POOL_WINDOWS) * POOL_GROUP_W
POOL_BUF = 15
POOL_HALO = 16
MOBA_BLOCK = 256
MOBA_TOPK = 3
GDN_CHUNK = 64
GDN_CONV = 4
FFN_CONV = 3
ROPE_THETA = 10000.0
EPS = 1e-6
SEQ_PAD = 8
NEG = -1e30

COL_QKV_A = 0
COL_QKV_B = 3 * ATT_W
COL_POOL = COL_QKV_B + 3 * GDN_W
COL_AB = COL_POOL + POOL_W
COL_Z = COL_AB + 256
IN_W_PAD = COL_Z + GDN_W


def _cparams(sem):
    return pltpu.CompilerParams(dimension_semantics=sem, vmem_limit_bytes=VMEM_LIMIT_BYTES)


def _tile(n, pref):
    t = min(n, pref)
    while n % t:
        t -= SUBLANES
    assert t > 0 and n % t == 0
    return t


def _sigmoid(x):
    return 1.0 / (1.0 + jnp.exp(-x))


def _silu(x):
    return x * _sigmoid(x)


def _softplus(x):
    return jnp.maximum(x, 0.0) + jnp.log(1.0 + jnp.exp(-jnp.abs(x)))


def _dot_nt(a, b, precision=None):
    return lax.dot_general(a, b, (((1,), (1,)), ((), ())), precision=precision,
                           preferred_element_type=f32)


def _dot_tn(a, b, precision=None):
    return lax.dot_general(a, b, (((0,), (0,)), ((), ())), precision=precision,
                           preferred_element_type=f32)


def _mod_spec(mod, tm, ncols):
    if mod.shape[0] == 1:
        return pl.BlockSpec((1, ncols), lambda i, *_: (0, 0))
    return pl.BlockSpec((tm, ncols), lambda i, *_: (i, 0))


def _ada_body(c_ref, w_ref, b_ref, o_ref):
    s = _silu(c_ref[...])
    o_ref[...] = jnp.dot(s.astype(bf16), w_ref[...].astype(bf16),
                         preferred_element_type=f32) + b_ref[...]


def ada_mod(c_all, w_ada, b_ada):
    mc, d = c_all.shape
    depth, _, n = w_ada.shape
    tn = _tile(n, 1024)
    return pl.pallas_call(
        _ada_body,
        grid=(depth, n // tn),
        in_specs=[pl.BlockSpec((mc, d), lambda l, j: (0, 0)),
                  pl.BlockSpec((None, d, tn), lambda l, j: (l, 0, j)),
                  pl.BlockSpec((None, 1, tn), lambda l, j: (l, 0, j))],
        out_specs=pl.BlockSpec((None, mc, tn), lambda l, j: (l, 0, j)),
        out_shape=jax.ShapeDtypeStruct((depth, mc, n), f32),
        compiler_params=_cparams(("arbitrary", "arbitrary")),
        name="ada_mod",
    )(c_all, w_ada, b_ada.reshape(depth, 1, n))


def _nmm_body(x_ref, g_ref, sc_ref, sh_ref, *rest, n_w):
    w_refs, o_refs, h_scr = rest[:n_w], rest[n_w:2 * n_w], rest[2 * n_w]

    @pl.when(pl.program_id(1) == 0)
    def _():
        x = x_ref[...]
        y = x * lax.rsqrt(jnp.mean(x * x, axis=-1, keepdims=True) + EPS)
        h = (y * g_ref[...]) * (1.0 + sc_ref[...]) + sh_ref[...]
        h_scr[...] = h.astype(bf16)

    h = h_scr[...]
    for w_ref, o_ref in zip(w_refs, o_refs):
        o_ref[...] = jnp.dot(h, w_ref[...].astype(bf16), preferred_element_type=f32)


def norm_mod_matmul(x, g, sc, sh, ws, *, tm_pref=512, tn_pref=512, name="nmm"):
    m, d = x.shape
    n = ws[0].shape[1]
    tm, tn = _tile(m, tm_pref), _tile(n, tn_pref)
    n_w = len(ws)
    outs = pl.pallas_call(
        functools.partial(_nmm_body, n_w=n_w),
        grid=(m // tm, n // tn),
        in_specs=[pl.BlockSpec((tm, d), lambda i, j: (i, 0)),
                  pl.BlockSpec((1, d), lambda i, j: (0, 0)),
                  _mod_spec(sc, tm, d), _mod_spec(sh, tm, d)]
                 + [pl.BlockSpec((d, tn), lambda i, j: (0, j)) for _ in ws],
        out_specs=[pl.BlockSpec((tm, tn), lambda i, j: (i, j)) for _ in ws],
        out_shape=[jax.ShapeDtypeStruct((m, n), f32) for _ in ws],
        scratch_shapes=[pltpu.VMEM((tm, d), bf16)],
        compiler_params=_cparams(("arbitrary", "arbitrary")),
        name=name,
    )(x, g.reshape(1, d), sc, sh, *ws)
    return outs


def _mmr_body(a_ref, w_ref, x_ref, gt_ref, o_ref, acc, *, nk):
    k = pl.program_id(1)

    @pl.when(k == 0)
    def _():
        acc[...] = jnp.zeros_like(acc)

    acc[...] += jnp.dot(a_ref[...].astype(bf16), w_ref[...].astype(bf16),
                        preferred_element_type=f32)

    @pl.when(k == nk - 1)
    def _():
        o_ref[...] = x_ref[...] + gt_ref[...] * acc[...]


def matmul_residual(a, w, x, gt, *, tm_pref=512, tk_pref=512, name="mmr"):
    m, kdim = a.shape
    n = w.shape[1]
    tm, tk = _tile(m, tm_pref), _tile(kdim, tk_pref)
    nk = kdim // tk
    return pl.pallas_call(
        functools.partial(_mmr_body, nk=nk),
        grid=(m // tm, nk),
        in_specs=[pl.BlockSpec((tm, tk), lambda i, k: (i, k)),
                  pl.BlockSpec((tk, n), lambda i, k: (k, 0)),
                  pl.BlockSpec((tm, n), lambda i, k: (i, 0)),
                  _mod_spec(gt, tm, n)],
        out_specs=pl.BlockSpec((tm, n), lambda i, k: (i, 0)),
        out_shape=jax.ShapeDtypeStruct((m, n), f32),
        scratch_shapes=[pltpu.VMEM((tm, n), f32)],
        compiler_params=_cparams(("arbitrary", "arbitrary")),
        name=name,
    )(a, w, x, gt)


def _rms_body(x_ref, g_ref, o_ref):
    x = x_ref[...]
    o_ref[...] = x * lax.rsqrt(jnp.mean(x * x, axis=-1, keepdims=True) + EPS) * g_ref[...]


def rms_norm(x, g):
    m, d = x.shape
    tm = _tile(m, 512)
    return pl.pallas_call(
        _rms_body,
        grid=(m // tm,),
        in_specs=[pl.BlockSpec((tm, d), lambda i: (i, 0)), pl.BlockSpec((1, d), lambda i: (0, 0))],
        out_specs=pl.BlockSpec((tm, d), lambda i: (i, 0)),
        out_shape=jax.ShapeDtypeStruct((m, d), f32),
        compiler_params=_cparams(("arbitrary",)),
        name="final_norm",
    )(x, g.reshape(1, d))


def _rope(x, cos, sin_signed):
    return x * cos + pltpu.roll(x, HEAD_DIM // 2, axis=1) * sin_signed


def _attn_prep_body(q_ref, k_ref, v_ref, cos_ref, sin_ref, qo, ko, vo, *kmo):
    cos, sin = cos_ref[...], sin_ref[...]
    qo[...] = _rope(q_ref[...], cos, sin)
    kr = _rope(k_ref[...], cos, sin)
    ko[...] = kr
    vo[...] = v_ref[...]
    if kmo:
        kmo[0][...] = jnp.mean(kr, axis=0, keepdims=True)


def attn_prep_prompt(proj, cos, sin):
    t = proj.shape[0]
    assert t % MOBA_BLOCK == 0
    nb = t // MOBA_BLOCK
    h_n = N_ATT_HEADS
    hm = jax.ShapeDtypeStruct((h_n, t, HEAD_DIM), f32)
    col = lambda off: pl.BlockSpec((MOBA_BLOCK, HEAD_DIM), lambda i, h: (i, off + h))
    hm_spec = pl.BlockSpec((None, MOBA_BLOCK, HEAD_DIM), lambda i, h: (h, i, 0))
    q, k, v, km = pl.pallas_call(
        _attn_prep_body,
        grid=(nb, h_n),
        in_specs=[col(0), col(h_n), col(2 * h_n),
                  pl.BlockSpec((MOBA_BLOCK, HEAD_DIM), lambda i, h: (i, 0)),
                  pl.BlockSpec((MOBA_BLOCK, HEAD_DIM), lambda i, h: (i, 0))],
        out_specs=[hm_spec, hm_spec, hm_spec,
                   pl.BlockSpec((None, None, 1, HEAD_DIM), lambda i, h: (h, i, 0, 0))],
        out_shape=[hm, hm, hm, jax.ShapeDtypeStruct((h_n, nb, 1, HEAD_DIM), f32)],
        compiler_params=_cparams(("arbitrary", "arbitrary")),
        name="attn_prep_prompt",
    )(proj, proj, proj, cos, sin)
    return q, k, v, km.reshape(h_n, nb, HEAD_DIM)


def attn_prep_sample(proj, cos, sin, n_seq):
    h_n = N_ATT_HEADS
    shp = jax.ShapeDtypeStruct((n_seq, h_n, SEQ_PAD, HEAD_DIM), f32)
    col = lambda off: pl.BlockSpec((SEQ_PAD, HEAD_DIM), lambda b, h: (b, off + h))
    o_spec = pl.BlockSpec((None, None, SEQ_PAD, HEAD_DIM), lambda b, h: (b, h, 0, 0))
    tab = pl.BlockSpec((SEQ_PAD, HEAD_DIM), lambda b, h: (0, 0))
    return pl.pallas_call(
        _attn_prep_body,
        grid=(n_seq, h_n),
        in_specs=[col(0), col(h_n), col(2 * h_n), tab, tab],
        out_specs=[o_spec, o_spec, o_spec],
        out_shape=[shp, shp, shp],
        compiler_params=_cparams(("arbitrary", "arbitrary")),
        name="attn_prep_sample",
    )(proj, proj, proj, cos, sin)


def _top_blocks(gate, n_valid, on_pick):
    lane = lax.broadcasted_iota(jnp.int32, gate.shape, 1)
    g = jnp.where(lane < n_valid, gate, -jnp.inf)
    for r in range(MOBA_TOPK):
        m = jnp.max(g, axis=1, keepdims=True)
        idx = jnp.min(jnp.where(g == m, lane, LANES), axis=1, keepdims=True)
        on_pick(r, idx, m > -jnp.inf)
        g = jnp.where(lane == idx, -jnp.inf, g)


def _attn_prompt_body(q_ref, k_ref, v_ref, km_ref, o_ref):
    i = pl.program_id(1)
    blk = MOBA_BLOCK
    q = q_ref[...]
    gate = _dot_nt(q, km_ref[...], HIGHEST)
    lane = lax.broadcasted_iota(jnp.int32, gate.shape, 1)
    sel = [jnp.zeros(gate.shape, f32)]

    def on_pick(r, idx, has):
        sel[0] = jnp.where(lane == idx, jnp.where(has, 1.0, sel[0]), sel[0])

    _top_blocks(gate, i, on_pick)
    sel = sel[0]
    qb = (q * (HEAD_DIM ** -0.5)).astype(bf16)

    def scores(j):
        start = pl.multiple_of(j * blk, blk)
        kj = k_ref[pl.ds(start, blk), :].astype(bf16)
        vj = v_ref[pl.ds(start, blk), :].astype(bf16)
        return _dot_nt(qb, kj), vj

    def update(carry, s, vj):
        m_i, l_i, acc = carry
        m_new = jnp.maximum(m_i, jnp.max(s, axis=1, keepdims=True))
        alpha = jnp.exp(m_i - m_new)
        p = jnp.exp(s - m_new)
        l_new = alpha * l_i + jnp.sum(p, axis=1, keepdims=True)
        acc = alpha * acc + jnp.dot(p.astype(bf16), vj, preferred_element_type=f32)
        return m_new, l_new, acc

    def past_block(j, carry):
        s, vj = scores(j)
        picked = jnp.sum(jnp.where(lane == j, sel, 0.0), axis=1, keepdims=True)
        return update(carry, jnp.where(picked > 0.0, s, NEG), vj)

    init = (jnp.full((blk, 1), NEG, f32), jnp.zeros((blk, 1), f32), jnp.zeros((blk, HEAD_DIM), f32))
    carry = lax.fori_loop(0, i, past_block, init)
    s, vj = scores(i)
    row = lax.broadcasted_iota(jnp.int32, (blk, blk), 0)
    colk = lax.broadcasted_iota(jnp.int32, (blk, blk), 1)
    _, l_i, acc = update(carry, jnp.where(colk <= row, s, NEG), vj)
    o_ref[...] = acc / l_i


def attn_prompt(q, k, v, km):
    h_n, t, _ = q.shape
    nb = t // MOBA_BLOCK
    assert nb <= LANES
    km_pad = jnp.pad(km, ((0, 0), (0, LANES - nb), (0, 0)))
    full = pl.BlockSpec((None, t, HEAD_DIM), lambda h, i: (h, 0, 0))
    return pl.pallas_call(
        _attn_prompt_body,
        grid=(h_n, nb),
        in_specs=[pl.BlockSpec((None, MOBA_BLOCK, HEAD_DIM), lambda h, i: (h, i, 0)), full, full,
                  pl.BlockSpec((None, LANES, HEAD_DIM), lambda h, i: (h, 0, 0))],
        out_specs=pl.BlockSpec((MOBA_BLOCK, HEAD_DIM), lambda h, i: (i, h)),
        out_shape=jax.ShapeDtypeStruct((t, h_n * HEAD_DIM), f32),
        compiler_params=_cparams(("arbitrary", "arbitrary")),
        name="attn_prompt",
    )(q, k, v, km_pad)


def _sample_select_body(pt_ref, *refs, n_blk, ppb):
    k_refs, q_ref, sel_ref, km_scr = refs[:ppb], refs[ppb], refs[ppb + 1], refs[ppb + 2]
    j = pl.program_id(1)

    @pl.when(j == 0)
    def _():
        km_scr[...] = jnp.zeros_like(km_scr)

    ks = k_refs[0][...].sum(axis=1)
    for k_ref in k_refs[1:]:
        ks = ks + k_ref[...].sum(axis=1)
    ks = ks * (1.0 / MOBA_BLOCK)
    for h in range(N_ATT_HEADS):
        km_scr[h, pl.ds(j, 1), :] = ks[h:h + 1, :]

    @pl.when(j == n_blk - 1)
    def _():
        lane = lax.broadcasted_iota(jnp.int32, (SEQ_PAD, LANES), 1)
        sel = [jnp.zeros((SEQ_PAD, LANES), jnp.int32)]
        for h in range(N_ATT_HEADS):
            gate = _dot_nt(q_ref[h], km_scr[h], HIGHEST)

            def on_pick(r, idx, has, h=h):
                sel[0] = jnp.where(lane == h * MOBA_TOPK + r, idx, sel[0])

            _top_blocks(gate, n_blk, on_pick)
        sel_ref[...] = sel[0]


def sample_select(cache_k_hm, layer, page_table, q_s):
    n_seq, n_pages = page_table.shape
    page = cache_k_hm.shape[3]
    ppb = MOBA_BLOCK // page
    assert ppb * page == MOBA_BLOCK and n_pages % ppb == 0
    n_blk = n_pages // ppb
    assert MOBA_TOPK <= n_blk <= LANES
    h_n = N_ATT_HEADS

    def page_spec(p):
        return pl.BlockSpec((None, None, h_n, page, HEAD_DIM),
                            lambda b, j, pt: (layer, pt[b, j * ppb + p], 0, 0, 0))

    return pl.pallas_call(
        functools.partial(_sample_select_body, n_blk=n_blk, ppb=ppb),
        grid_spec=pltpu.PrefetchScalarGridSpec(
            num_scalar_prefetch=1, grid=(n_seq, n_blk),
            in_specs=[page_spec(p) for p in range(ppb)]
                     + [pl.BlockSpec((None, h_n, SEQ_PAD, HEAD_DIM), lambda b, j, pt: (b, 0, 0, 0))],
            out_specs=pl.BlockSpec((None, SEQ_PAD, LANES), lambda b, j, pt: (b, 0, 0)),
            scratch_shapes=[pltpu.VMEM((h_n, LANES, HEAD_DIM), f32)]),
        out_shape=jax.ShapeDtypeStruct((n_seq, SEQ_PAD, LANES), jnp.int32),
        compiler_params=_cparams(("arbitrary", "arbitrary")),
        name="sample_select",
    )(page_table, *([cache_k_hm] * ppb), q_s)


def _sample_attn_body(pt_ref, sel_ref, ck_ref, cv_ref, q_ref, kn_ref, vn_ref, o_ref,
                      kbuf, vbuf, sem, *, layer, t_new, ppb, page):
    b = pl.program_id(0)
    h_n = N_ATT_HEADS
    n_sel = MOBA_TOPK * MOBA_BLOCK

    def copies(t, h):
        out = []
        for r in range(MOBA_TOPK):
            blk = sel_ref[b, (t * h_n + h) * MOBA_TOPK + r]
            for p in range(ppb):
                pg = pt_ref[b, blk * ppb + p]
                dst = pl.ds((r * ppb + p) * page, page)
                th = t * h_n + h
                out.append(pltpu.make_async_copy(ck_ref.at[layer, pg, h], kbuf.at[th, dst], sem.at[th, 0]))
                out.append(pltpu.make_async_copy(cv_ref.at[layer, pg, h], vbuf.at[th, dst], sem.at[th, 1]))
        return out

    for t in range(t_new):
        for h in range(h_n):
            for c in copies(t, h):
                c.start()

    o_ref[...] = jnp.zeros_like(o_ref)
    rown = lax.broadcasted_iota(jnp.int32, (SEQ_PAD, 1), 0)
    for t in range(t_new):
        for h in range(h_n):
            for c in copies(t, h):
                c.wait()
            qrow = q_ref[h, t:t + 1, :] * (HEAD_DIM ** -0.5)
            s = jnp.sum(kbuf[t * h_n + h] * qrow, axis=1, keepdims=True)
            s_new = jnp.sum(kn_ref[h] * qrow, axis=1, keepdims=True)
            s_new = jnp.where(rown <= t, s_new, NEG)
            m = jnp.maximum(jnp.max(s, axis=0, keepdims=True), jnp.max(s_new, axis=0, keepdims=True))
            e = jnp.exp(s - m)
            e_new = jnp.exp(s_new - m)
            den = jnp.sum(e, axis=0, keepdims=True) + jnp.sum(e_new, axis=0, keepdims=True)
            num = (jnp.sum(e * vbuf[t * h_n + h], axis=0, keepdims=True)
                   + jnp.sum(e_new * vn_ref[h], axis=0, keepdims=True))
            o_ref[t:t + 1, h * HEAD_DIM:(h + 1) * HEAD_DIM] = num / den


def sample_attention(cache_k_hm, cache_v_hm, layer, page_table, sel, q_s, k_s, v_s, t_new):
    n_seq = page_table.shape[0]
    page = cache_k_hm.shape[3]
    ppb = MOBA_BLOCK // page
    h_n = N_ATT_HEADS
    n_sel = MOBA_TOPK * MOBA_BLOCK
    any_spec = pl.BlockSpec(memory_space=pl.ANY)
    seq_spec = pl.BlockSpec((None, h_n, SEQ_PAD, HEAD_DIM), lambda b, pt, sl: (b, 0, 0, 0))
    return pl.pallas_call(
        functools.partial(_sample_attn_body, layer=layer, t_new=t_new, ppb=ppb, page=page),
        grid_spec=pltpu.PrefetchScalarGridSpec(
            num_scalar_prefetch=2, grid=(n_seq,),
            in_specs=[any_spec, any_spec, seq_spec, seq_spec, seq_spec],
            out_specs=pl.BlockSpec((None, SEQ_PAD, h_n * HEAD_DIM), lambda b, pt, sl: (b, 0, 0)),
            scratch_shapes=[pltpu.VMEM((t_new * h_n, n_sel, HEAD_DIM), f32),
                            pltpu.VMEM((t_new * h_n, n_sel, HEAD_DIM), f32),
                            pltpu.SemaphoreType.DMA((t_new * h_n, 2))]),
        out_shape=jax.ShapeDtypeStruct((n_seq, SEQ_PAD, h_n * HEAD_DIM), f32),
        compiler_params=_cparams(("arbitrary",)),
        name="sample_attention",
    )(page_table, sel, cache_k_hm, cache_v_hm, q_s, k_s, v_s)


def _halo_cat(prev_ref, init_ref, x_ref):
    halo = jnp.where(pl.program_id(1) == 0, init_ref[...], prev_ref[...])
    return jnp.concatenate([halo, x_ref[...]], axis=0)


def _halo_specs(n_tiles, tm, halo, width, x_col, init_col):
    assert tm % halo == 0 or n_tiles == 1
    per = tm // halo

    def prev_map(b, i, *r):
        return (jnp.maximum((b * n_tiles + i) * per - 1, 0), x_col(*r))

    return [pl.BlockSpec((halo, width), prev_map),
            pl.BlockSpec((None, halo, width), lambda b, i, *r: (b, 0, init_col(*r))),
            pl.BlockSpec((tm, width), lambda b, i, *r: (b * n_tiles + i, x_col(*r)))]


def _gdn_prep_body(prev_ref, init_ref, x_ref, ab_ref, cw_ref, alog_ref, dtb_ref,
                   q_o, k_o, v_o, gb_o, *, tm, t_valid):
    cat = _halo_cat(prev_ref, init_ref, x_ref)
    w = cw_ref[...]
    acc = x_ref[...] * w[GDN_CONV - 1:GDN_CONV, :]
    for s in range(1, GDN_CONV):
        acc = acc + pltpu.roll(cat, s, axis=0)[SUBLANES:] * w[GDN_CONV - 1 - s:GDN_CONV - s, :]
    y = _silu(acc)
    row = pl.program_id(1) * tm + lax.broadcasted_iota(jnp.int32, (tm, 1), 0)
    valid = row < t_valid
    for h in range(N_GDN_HEADS):
        lo = h * HEAD_DIM
        qh = y[:, lo:lo + HEAD_DIM]
        kh = y[:, GDN_W + lo:GDN_W + lo + HEAD_DIM]
        qn = qh * lax.rsqrt(jnp.sum(qh * qh, axis=-1, keepdims=True) + EPS) * (HEAD_DIM ** -0.5)
        kn = kh * lax.rsqrt(jnp.sum(kh * kh, axis=-1, keepdims=True) + EPS)
        q_o[:, lo:lo + HEAD_DIM] = jnp.where(valid, qn, 0.0)
        k_o[:, lo:lo + HEAD_DIM] = jnp.where(valid, kn, 0.0)
    v_o[...] = jnp.where(valid, y[:, 2 * GDN_W:], 0.0)
    ab = ab_ref[...]
    g = -jnp.exp(alog_ref[...]) * _softplus(ab + dtb_ref[...])
    lane = lax.broadcasted_iota(jnp.int32, ab.shape, 1)
    gb = jnp.where(lane < N_GDN_HEADS, g, jnp.where(lane < 2 * N_GDN_HEADS, _sigmoid(ab), 0.0))
    gb_o[...] = jnp.where(valid, gb, 0.0)


def gdn_prep(proj, init, conv_w, a_log, dt_bias, *, n_seq, t_pad, t_valid, tm_pref=256):
    m = proj.shape[0]
    tm = _tile(t_pad, tm_pref)
    n_tiles = t_pad // tm
    w3 = 3 * GDN_W
    pad = lambda a: jnp.pad(a.reshape(1, -1), ((0, 0), (0, LANES - a.shape[-1])))
    ab_blk = COL_AB // LANES
    out = jax.ShapeDtypeStruct((m, GDN_W), f32)
    row_spec = lambda wd: pl.BlockSpec((tm, wd), lambda b, i: (b * n_tiles + i, 0))
    const = lambda r, c: pl.BlockSpec((r, c), lambda b, i: (0, 0))
    return pl.pallas_call(
        functools.partial(_gdn_prep_body, tm=tm, t_valid=t_valid),
        grid=(n_seq, n_tiles),
        in_specs=_halo_specs(n_tiles, tm, SUBLANES, w3, lambda: COL_QKV_B // w3, lambda: 0)
                 + [pl.BlockSpec((tm, LANES), lambda b, i: (b * n_tiles + i, ab_blk)),
                    const(GDN_CONV, w3), const(1, LANES), const(1, LANES)],
        out_specs=[row_spec(GDN_W), row_spec(GDN_W), row_spec(GDN_W), row_spec(LANES)],
        out_shape=[out, out, out, jax.ShapeDtypeStruct((m, LANES), f32)],
        compiler_params=_cparams(("arbitrary", "arbitrary")),
        name="gdn_prep",
    )(proj, init, proj, proj, conv_w, pad(a_log), pad(dt_bias))


def _gdn_body(q_ref, k_ref, v_ref, gb_ref, z_ref, nw_ref, s0_ref, o_ref, s_out_ref, s_scr, *, n_chunks):
    c = pl.program_id(1)
    cs = GDN_CHUNK

    @pl.when(c == 0)
    def _():
        s_scr[...] = s0_ref[...]

    gb = gb_ref[...]
    r = lax.broadcasted_iota(jnp.int32, (cs, cs), 0)
    cc = lax.broadcasted_iota(jnp.int32, (cs, cs), 1)
    causal, strict = r >= cc, r > cc
    eye = jnp.where(r == cc, 1.0, 0.0)
    gcum = jnp.dot(jnp.where(causal, 1.0, 0.0), gb, precision=HIGHEST, preferred_element_type=f32)
    for h in range(N_GDN_HEADS):
        lo = h * HEAD_DIM
        q, k, v = q_ref[:, lo:lo + HEAD_DIM], k_ref[:, lo:lo + HEAD_DIM], v_ref[:, lo:lo + HEAD_DIM]
        gcol = gcum[:, h:h + 1]
        bcol = gb[:, N_GDN_HEADS + h:N_GDN_HEADS + h + 1]
        gi = jnp.broadcast_to(gcol, (cs, cs))
        grow = jnp.sum(jnp.where(r == cc, gi, 0.0), axis=0, keepdims=True)
        decay = jnp.exp(jnp.where(causal, gi - grow, NEG))
        a = jnp.where(strict, bcol * _dot_nt(k, k, HIGHEST) * decay, 0.0)
        x = eye - a
        p = jnp.dot(a, a, precision=HIGHEST, preferred_element_type=f32)
        n_sq = max(1, (cs - 1).bit_length() - 1)
        for it in range(n_sq):
            x = x + jnp.dot(x, p, precision=HIGHEST, preferred_element_type=f32)
            if it + 1 < n_sq:
                p = jnp.dot(p, p, precision=HIGHEST, preferred_element_type=f32)
        eg = jnp.exp(gcol)
        rhs = jnp.concatenate([v * bcol, k * (bcol * eg)], axis=1)
        sol = jnp.dot(x, rhs, precision=HIGHEST, preferred_element_type=f32)
        u, w = sol[:, :HEAD_DIM], sol[:, HEAD_DIM:]
        qk = jnp.where(causal, _dot_nt(q, k, HIGHEST) * decay, 0.0)
        s = s_scr[h]
        v_new = u - jnp.dot(w, s, precision=HIGHEST, preferred_element_type=f32)
        o = (jnp.dot(q * eg, s, precision=HIGHEST, preferred_element_type=f32)
             + jnp.dot(qk, v_new, precision=HIGHEST, preferred_element_type=f32))
        g_last = gcol[cs - 1:cs, :]
        s_scr[h] = s * jnp.exp(g_last) + _dot_tn(k * jnp.exp(g_last - gcol), v_new, HIGHEST)
        z = z_ref[:, lo:lo + HEAD_DIM]
        on = o * lax.rsqrt(jnp.mean(o * o, axis=-1, keepdims=True) + EPS) * nw_ref[...]
        o_ref[:, lo:lo + HEAD_DIM] = on * _silu(z)

    @pl.when(c == n_chunks - 1)
    def _():
        s_out_ref[...] = s_scr[...]


def gdn_recurrence(q, k, v, gb, z, z_blk, norm_w, s0, *, n_seq, t_pad):
    m = q.shape[0]
    cs = GDN_CHUNK
    assert t_pad % cs == 0
    n_chunks = t_pad // cs
    row = lambda wd, cb=0: pl.BlockSpec((cs, wd), lambda b, c: (b * n_chunks + c, cb))
    st = pl.BlockSpec((None, N_GDN_HEADS, HEAD_DIM, HEAD_DIM), lambda b, c: (b, 0, 0, 0))
    return pl.pallas_call(
        functools.partial(_gdn_body, n_chunks=n_chunks),
        grid=(n_seq, n_chunks),
        in_specs=[row(GDN_W), row(GDN_W), row(GDN_W), row(LANES), row(GDN_W, z_blk),
                  pl.BlockSpec((1, HEAD_DIM), lambda b, c: (0, 0)), st],
        out_specs=[row(GDN_W), st],
        out_shape=[jax.ShapeDtypeStruct((m, GDN_W), f32),
                   jax.ShapeDtypeStruct((n_seq, N_GDN_HEADS, HEAD_DIM, HEAD_DIM), f32)],
        scratch_shapes=[pltpu.VMEM((N_GDN_HEADS, HEAD_DIM, HEAD_DIM), f32)],
        compiler_params=_cparams(("arbitrary", "arbitrary")),
        name="gdn_recurrence",
    )(q, k, v, gb, z, norm_w.reshape(1, HEAD_DIM), s0)


def _pool_body(prev_ref, init_ref, x_ref, pw_ref, ps_ref, o_ref, *, tm, start):
    cat = _halo_cat(prev_ref, init_ref, x_ref)
    sums, s, span = [], cat, 1
    for _ in POOL_WINDOWS:
        s = s + pltpu.roll(s, span, axis=0)
        span *= 2
        sums.append(s)
    assert span == POOL_HALO
    pos = start + pl.program_id(1) * tm + lax.broadcasted_iota(jnp.int32, (tm, 1), 0)
    x = x_ref[...]
    for g, win in enumerate(POOL_WINDOWS):
        lo = g * POOL_GROUP_W
        cnt = jnp.minimum(win, pos + 1).astype(f32)
        mix = sums[g][POOL_HALO:, lo:lo + POOL_GROUP_W] / cnt - x[:, lo:lo + POOL_GROUP_W]
        out = jnp.dot(mix, pw_ref[g], precision=HIGHEST, preferred_element_type=f32)
        o_ref[:, lo:lo + POOL_GROUP_W] = out * ps_ref[:, lo:lo + POOL_GROUP_W]


def pool_mix(proj, init, pool_w, pool_scale, *, n_seq, t_pad, start, tm_pref=512):
    m = proj.shape[0]
    tm = _tile(t_pad, tm_pref)
    n_tiles = t_pad // tm
    return pl.pallas_call(
        functools.partial(_pool_body, tm=tm, start=start),
        grid=(n_seq, n_tiles),
        in_specs=_halo_specs(n_tiles, tm, POOL_HALO, POOL_W, lambda: COL_POOL // POOL_W, lambda: 0)
                 + [pl.BlockSpec(pool_w.shape, lambda b, i: (0, 0, 0)),
                    pl.BlockSpec((1, POOL_W), lambda b, i: (0, 0))],
        out_specs=pl.BlockSpec((tm, POOL_W), lambda b, i: (b * n_tiles + i, 0)),
        out_shape=jax.ShapeDtypeStruct((m, POOL_W), f32),
        compiler_params=_cparams(("arbitrary", "arbitrary")),
        name="pool_mix",
    )(proj, init, proj, pool_w, pool_scale.reshape(1, POOL_W))


def _conv_act_body(prev_ref, init_ref, a_ref, u_ref, cw_ref, o_ref):
    cat = _halo_cat(prev_ref, init_ref, a_ref)
    w = cw_ref[...]
    acc = a_ref[...] * w[FFN_CONV - 1:FFN_CONV, :]
    for s in range(1, FFN_CONV):
        acc = acc + pltpu.roll(cat, s, axis=0)[SUBLANES:] * w[FFN_CONV - 1 - s:FFN_CONV - s, :]
    o_ref[...] = _silu(acc) * u_ref[...]


def conv_act(a, u, init, conv_w, *, n_seq, t_pad, tm_pref=512, tf_pref=512):
    m, f = a.shape
    tm, tf = _tile(t_pad, tm_pref), _tile(f, tf_pref)
    n_tiles = t_pad // tm
    return pl.pallas_call(
        _conv_act_body,
        grid=(n_seq, n_tiles, f // tf),
        in_specs=_halo_specs(n_tiles, tm, SUBLANES, tf, lambda j: j, lambda j: j)
                 + [pl.BlockSpec((tm, tf), lambda b, i, j: (b * n_tiles + i, j)),
                    pl.BlockSpec((FFN_CONV, tf), lambda b, i, j: (0, j))],
        out_specs=pl.BlockSpec((tm, tf), lambda b, i, j: (b * n_tiles + i, j)),
        out_shape=jax.ShapeDtypeStruct((m, f), f32),
        compiler_params=_cparams(("arbitrary", "arbitrary", "arbitrary")),
        name="conv_act",
    )(a, init, a, u, conv_w)


def _rope_tables(pos):
    half = HEAD_DIM // 2
    inv = ROPE_THETA ** (-jnp.arange(half, dtype=f32) * 2.0 / HEAD_DIM)
    ang = pos.astype(f32)[:, None] * inv[None, :]
    cos, sin = jnp.cos(ang), jnp.sin(ang)
    return jnp.concatenate([cos, cos], axis=-1), jnp.concatenate([-sin, sin], axis=-1)


def _reorder_w_in(w_in):
    qkv = 3 * ATT_W + 3 * GDN_W
    z_lo = qkv
    ab_lo = z_lo + GDN_W
    pool_lo = ab_lo + 2 * N_GDN_HEADS
    assert w_in.shape[-1] == pool_lo + POOL_W
    zeros = jnp.zeros(w_in.shape[:-1] + (COL_Z - COL_AB - 2 * N_GDN_HEADS,), w_in.dtype)
    w = jnp.concatenate([w_in[..., :qkv], w_in[..., pool_lo:], w_in[..., ab_lo:pool_lo], zeros,
                         w_in[..., z_lo:ab_lo]], axis=-1)
    return w.astype(bf16)


def _tail_rows(buf, x, n):
    t = x.shape[1]
    if t >= n:
        return x[:, t - n:]
    return jnp.concatenate([buf[:, t:], x], axis=1)


def _front_pad(state, rows):
    return jnp.pad(state, ((0, 0), (rows - state.shape[1], 0), (0, 0)))


def _group_layer(x, mods, lw, *, n_seq, t_pad, t_valid, start, states, attn_fn):
    (g_mix, g_ffn, w_in_r, gdn_conv_w, gdn_a_log, gdn_dt_bias, gdn_norm_w, pool_w, pool_scale,
     w_out, w_gate, w_up, ffn_conv_w, w_down) = lw
    sh1, sc1, gt1, sh2, sc2, gt2 = mods
    s_gdn, gdn_buf, pool_buf, ffn_buf = states
    m = x.shape[0]
    (proj,) = norm_mod_matmul(x, g_mix, sc1, sh1, [w_in_r], name="in_proj")

    o_a, k_new, v_new = attn_fn(proj)

    qn, kn, vc, gb = gdn_prep(proj, _front_pad(gdn_buf, SUBLANES), gdn_conv_w, gdn_a_log, gdn_dt_bias,
                              n_seq=n_seq, t_pad=t_pad, t_valid=t_valid)
    t_gdn = -(-t_pad // GDN_CHUNK) * GDN_CHUNK
    if t_gdn == t_pad:
        o_b, s_new = gdn_recurrence(qn, kn, vc, gb, proj, COL_Z // GDN_W, gdn_norm_w, s_gdn,
                                    n_seq=n_seq, t_pad=t_pad)
    else:
        def chunk_pad(a):
            a = a.reshape(n_seq, t_pad, a.shape[-1])
            return jnp.pad(a, ((0, 0), (0, t_gdn - t_pad), (0, 0))).reshape(n_seq * t_gdn, -1)
        o_b, s_new = gdn_recurrence(chunk_pad(qn), chunk_pad(kn), chunk_pad(vc), chunk_pad(gb),
                                    chunk_pad(proj[:, COL_Z:COL_Z + GDN_W]), 0, gdn_norm_w, s_gdn,
                                    n_seq=n_seq, t_pad=t_gdn)
        o_b = o_b.reshape(n_seq, t_gdn, GDN_W)[:, :t_pad].reshape(m, GDN_W)
    qkv_b = proj[:, COL_QKV_B:COL_QKV_B + 3 * GDN_W].reshape(n_seq, t_pad, 3 * GDN_W)[:, :t_valid]
    new_gdn_buf = _tail_rows(gdn_buf, qkv_b, GDN_CONV - 1)

    o_p = pool_mix(proj, _front_pad(pool_buf, POOL_HALO), pool_w, pool_scale,
                   n_seq=n_seq, t_pad=t_pad, start=start)
    p_in = proj[:, COL_POOL:COL_POOL + POOL_W].reshape(n_seq, t_pad, POOL_W)[:, :t_valid]
    new_pool_buf = _tail_rows(pool_buf, p_in, POOL_BUF)

    x = matmul_residual(jnp.concatenate([o_a, o_b, o_p], axis=1), w_out, x, gt1, name="out_proj")

    a, u = norm_mod_matmul(x, g_ffn, sc2, sh2, [w_gate, w_up], name="ffn_up")
    act = conv_act(a, u, _front_pad(ffn_buf, SUBLANES), ffn_conv_w, n_seq=n_seq, t_pad=t_pad)
    new_ffn_buf = _tail_rows(ffn_buf, a.reshape(n_seq, t_pad, -1)[:, :t_valid], FFN_CONV - 1)
    x = matmul_residual(act, w_down, x, gt2, name="ffn_down")
    return x, k_new, v_new, s_new, new_gdn_buf, new_pool_buf, new_ffn_buf


def kernel(x_prompt, x_sample, cache_k, cache_v, page_table, state_gdn, state_gdn_conv, state_pool,
           state_ffn_conv, c_prompt, c_sample, w_ada, b_ada, g_mix, g_ffn, w_in, gdn_conv_w, gdn_a_log,
           gdn_dt_bias, gdn_norm_w, pool_w, pool_scale, w_out, w_gate, w_up, ffn_conv_w, w_down, g_final):
    bp, t_p, d = x_prompt.shape
    bs, t_s, _ = x_sample.shape
    depth = w_ada.shape[0]
    page = cache_k.shape[2]
    past_len = page_table.shape[1] * page
    assert bp == 1 and t_s <= SEQ_PAD and t_p % MOBA_BLOCK == 0 and past_len % MOBA_BLOCK == 0
    assert t_p >= POOL_BUF

    ck = jnp.transpose(cache_k, (0, 1, 3, 2, 4))
    cv = jnp.transpose(cache_v, (0, 1, 3, 2, 4))

    n_c = bp + bs
    mc = -(-n_c // SUBLANES) * SUBLANES
    c_all = jnp.pad(jnp.concatenate([c_prompt, c_sample], axis=0), ((0, mc - n_c), (0, 0)))
    mod = ada_mod(c_all, w_ada, b_ada)
    w_in_r = _reorder_w_in(w_in)

    cos_p, sin_p = _rope_tables(jnp.arange(t_p))
    cos_s, sin_s = _rope_tables(past_len + jnp.arange(SEQ_PAD))

    xp = x_prompt.reshape(bp * t_p, d)
    xs = jnp.pad(x_sample, ((0, 0), (0, SEQ_PAD - t_s), (0, 0))).reshape(bs * SEQ_PAD, d)

    zeros_p = lambda *shape: jnp.zeros((bp,) + shape, f32)
    outs_p, outs_s = [], []
    for l in range(depth):
        lw = (g_mix[l], g_ffn[l], w_in_r[l], gdn_conv_w[l], gdn_a_log[l], gdn_dt_bias[l], gdn_norm_w[l],
              pool_w[l], pool_scale[l], w_out[l], w_gate[l], w_up[l], ffn_conv_w[l], w_down[l])
        mods_p = [mod[l, 0:bp, i * d:(i + 1) * d] for i in range(6)]
        mods_s = [jnp.repeat(mod[l, bp:n_c, i * d:(i + 1) * d], SEQ_PAD, axis=0) for i in range(6)]

        def attn_p(proj):
            q, k, v, km = attn_prep_prompt(proj, cos_p, sin_p)
            return attn_prompt(q, k, v, km), k, v

        def attn_s(proj, l=l):
            q, k, v = attn_prep_sample(proj, cos_s, sin_s, bs)
            sel = sample_select(ck, l, page_table, q)
            sel = sel[:, :t_s, :N_ATT_HEADS * MOBA_TOPK].reshape(bs, -1)
            o = sample_attention(ck, cv, l, page_table, sel, q, k, v, t_s)
            return o.reshape(bs * SEQ_PAD, ATT_W), k[:, :, :t_s], v[:, :, :t_s]

        xp, *st_p = _group_layer(
            xp, mods_p, lw, n_seq=bp, t_pad=t_p, t_valid=t_p, start=0, attn_fn=attn_p,
            states=(zeros_p(N_GDN_HEADS, HEAD_DIM, HEAD_DIM), zeros_p(GDN_CONV - 1, 3 * GDN_W),
                    zeros_p(POOL_BUF, POOL_W), zeros_p(FFN_CONV - 1, w_gate.shape[-1])))
        xs, *st_s = _group_layer(
            xs, mods_s, lw, n_seq=bs, t_pad=SEQ_PAD, t_valid=t_s, start=past_len, attn_fn=attn_s,
            states=(state_gdn[l], state_gdn_conv[l], state_pool[l], state_ffn_conv[l]))
        outs_p.append(st_p)
        outs_s.append(st_s)

    y_prompt = rms_norm(xp, g_final).reshape(bp, t_p, d)
    y_sample = rms_norm(xs, g_final).reshape(bs, SEQ_PAD, d)[:, :t_s]
    kp, vp, gp, gcp, pp, fp = [jnp.stack(t) for t in zip(*outs_p)]
    ks, vs, gs, gcs, ps, fs = [jnp.stack(t) for t in zip(*outs_s)]
    k_prompt = jnp.transpose(kp, (0, 2, 1, 3))[:, None]
    v_prompt = jnp.transpose(vp, (0, 2, 1, 3))[:, None]
    k_sample = jnp.transpose(ks, (0, 1, 3, 2, 4))
    v_sample = jnp.transpose(vs, (0, 1, 3, 2, 4))
    return (y_prompt, y_sample, k_prompt, v_prompt, k_sample, v_sample, gp, gs, gcp, gcs, pp, ps, fp, fs)
```

```python
import functools

import jax
import jax.numpy as jnp
from jax import lax
from jax.experimental import pallas as pl
from jax.experimental.pallas import tpu as pltpu

f32 = jnp.float32
bf16 = jnp.bfloat16
HIGHEST = lax.Precision.HIGHEST

LANES = 128
SUBLANES = 8
VMEM_LIMIT_BYTES = 56 * 1024 * 1024

HEAD_DIM = 128
N_ATT_HEADS = 6
N_GDN_HEADS = 6
ATT_W = N_ATT_HEADS * HEAD_DIM
GDN_W = N_GDN_HEADS * HEAD_DIM
POOL_WINDOWS = (2, 4, 8, 16)
POOL_GROUP_W = 128
POOL_W = len(POOL_WINDOWS) * POOL_GROUP_W
POOL_BUF = 15
POOL_HALO = 16
MOBA_BLOCK = 256
MOBA_TOPK = 3
ATTN_STREAMS = 4
GDN_CHUNK = 64
GDN_TILE = 128
GDN_CONV = 4
FFN_CONV = 3
ROPE_THETA = 10000.0
EPS = 1e-6
SEQ_PAD = 8
NEG = -1e30

COL_QKV_A = 0
COL_QKV_B = 3 * ATT_W
COL_POOL = COL_QKV_B + 3 * GDN_W
COL_AB = COL_POOL + POOL_W
COL_Z = COL_AB + 256
IN_W_PAD = COL_Z + GDN_W


def _cparams(sem):
    return pltpu.CompilerParams(dimension_semantics=sem, vmem_limit_bytes=VMEM_LIMIT_BYTES)


def _tile(n, pref):
    t = min(n, pref)
    while n % t:
        t -= SUBLANES
    assert t > 0 and n % t == 0
    return t


def _act_dtype(tm):
    return bf16 if tm % (2 * SUBLANES) == 0 else f32


def _sigmoid(x):
    return 1.0 / (1.0 + jnp.exp(-x))


def _silu(x):
    return x * _sigmoid(x)


def _softplus(x):
    return jnp.maximum(x, 0.0) + jnp.log(1.0 + jnp.exp(-jnp.abs(x)))


def _dot_nt(a, b, precision=None):
    return lax.dot_general(a, b, (((1,), (1,)), ((), ())), precision=precision,
                           preferred_element_type=f32)


def _dot_tn(a, b, precision=None):
    return lax.dot_general(a, b, (((0,), (0,)), ((), ())), precision=precision,
                           preferred_element_type=f32)


def _bdot(a, b):
    return jnp.dot(a.astype(bf16), b.astype(bf16), preferred_element_type=f32)


def _split_bf16(a):
    hi = a.astype(bf16)
    return hi, (a - hi.astype(f32)).astype(bf16)


def _dot3(a, b):
    (ah, al), (bh, bl) = _split_bf16(a), _split_bf16(b)
    d = functools.partial(jnp.dot, preferred_element_type=f32)
    return d(ah, bh) + (d(ah, bl) + d(al, bh))


def _mod_spec(mod, tm, ncols):
    if mod.shape[0] == 1:
        return pl.BlockSpec((1, ncols), lambda i, *_: (0, 0))
    return pl.BlockSpec((tm, ncols), lambda i, *_: (i, 0))


def _ada_body(c_ref, w_ref, b_ref, o_ref):
    s = _silu(c_ref[...])
    o_ref[...] = jnp.dot(s.astype(bf16), w_ref[...].astype(bf16),
                         preferred_element_type=f32) + b_ref[...]


def ada_mod(c_all, w_ada, b_ada):
    mc, d = c_all.shape
    depth, _, n = w_ada.shape
    tn = _tile(n, 1024)
    return pl.pallas_call(
        _ada_body,
        grid=(depth, n // tn),
        in_specs=[pl.BlockSpec((mc, d), lambda l, j: (0, 0)),
                  pl.BlockSpec((None, d, tn), lambda l, j: (l, 0, j)),
                  pl.BlockSpec((None, 1, tn), lambda l, j: (l, 0, j))],
        out_specs=pl.BlockSpec((None, mc, tn), lambda l, j: (l, 0, j)),
        out_shape=jax.ShapeDtypeStruct((depth, mc, n), f32),
        compiler_params=_cparams(("arbitrary", "arbitrary")),
        name="ada_mod",
    )(c_all, w_ada, b_ada.reshape(depth, 1, n))


def _norm_mod_body(x_ref, g_ref, sc_ref, sh_ref, o_ref):
    x = x_ref[...]
    y = x * lax.rsqrt(jnp.mean(x * x, axis=-1, keepdims=True) + EPS)
    o_ref[...] = ((y * g_ref[...]) * (1.0 + sc_ref[...]) + sh_ref[...]).astype(o_ref.dtype)


def norm_mod(x, g, sc, sh, *, tm_pref=512, name="norm_mod"):
    m, d = x.shape
    tm = _tile(m, tm_pref)
    return pl.pallas_call(
        _norm_mod_body,
        grid=(m // tm,),
        in_specs=[pl.BlockSpec((tm, d), lambda i: (i, 0)), pl.BlockSpec((1, d), lambda i: (0, 0)),
                  _mod_spec(sc, tm, d), _mod_spec(sh, tm, d)],
        out_specs=pl.BlockSpec((tm, d), lambda i: (i, 0)),
        out_shape=jax.ShapeDtypeStruct((m, d), bf16),
        compiler_params=_cparams(("arbitrary",)),
        name=name,
    )(x, g.reshape(1, d), sc, sh)


def _mm_body(a_ref, *rest, n_w, residual, cast_w):
    w_refs, rest = rest[:n_w], rest[n_w:]
    if residual:
        (x_ref, gt_ref), rest = rest[:2], rest[2:]
    o_refs, w_scrs = rest[:n_w], rest[n_w:]

    if cast_w:
        @pl.when(pl.program_id(1) == 0)
        def _():
            for w_ref, w_scr in zip(w_refs, w_scrs):
                w_scr[...] = w_ref[...].astype(bf16)

    a = a_ref[...].astype(bf16)
    for n_i, o_ref in enumerate(o_refs):
        w = w_scrs[n_i][...] if cast_w else w_refs[n_i][...]
        acc = jnp.dot(a, w, preferred_element_type=f32)
        o_ref[...] = x_ref[...] + gt_ref[...] * acc if residual else acc


def matmul(a, ws, layer, *, x=None, gt=None, tm_pref=1024, tn_pref=512, name="mm"):
    m, kdim = a.shape
    n = ws[0].shape[2]
    tm, tn = _tile(m, tm_pref), _tile(n, tn_pref)
    n_w = len(ws)
    residual = x is not None
    assert not residual or n_w == 1
    cast_w = ws[0].dtype != bf16
    in_specs = [pl.BlockSpec((tm, kdim), lambda j, i: (i, 0))]
    in_specs += [pl.BlockSpec((None, kdim, tn), lambda j, i: (layer, 0, j)) for _ in ws]
    args = [a, *ws]
    if residual:
        gt_spec = (pl.BlockSpec((1, tn), lambda j, i: (0, j)) if gt.shape[0] == 1
                   else pl.BlockSpec((tm, tn), lambda j, i: (i, j)))
        in_specs += [pl.BlockSpec((tm, tn), lambda j, i: (i, j)), gt_spec]
        args += [x, gt]
    outs = pl.pallas_call(
        functools.partial(_mm_body, n_w=n_w, residual=residual, cast_w=cast_w),
        grid=(n // tn, m // tm),
        in_specs=in_specs,
        out_specs=[pl.BlockSpec((tm, tn), lambda j, i: (i, j)) for _ in ws],
        out_shape=[jax.ShapeDtypeStruct((m, n), f32) for _ in ws],
        scratch_shapes=[pltpu.VMEM((kdim, tn), bf16) for _ in ws] if cast_w else [],
        compiler_params=_cparams(("arbitrary", "arbitrary")),
        name=name,
    )(*args)
    return outs


def _ffn_up_conv_body(h_ref, wg_ref, wu_ref, init_ref, cw_ref, act_ref, tail_ref,
                      wg_scr, wu_scr, prev_scr, *, n_row_tiles):
    i = pl.program_id(1)

    @pl.when(i == 0)
    def _():
        wg_scr[...] = wg_ref[...].astype(bf16)
        wu_scr[...] = wu_ref[...].astype(bf16)
        prev_scr[...] = init_ref[...]

    h = h_ref[...]
    a = jnp.dot(h, wg_scr[...], preferred_element_type=f32)
    u = jnp.dot(h, wu_scr[...], preferred_element_type=f32)
    cat = jnp.concatenate([prev_scr[...], a], axis=0)
    w = cw_ref[...]
    acc = a * w[FFN_CONV - 1:FFN_CONV, :]
    for s in range(1, FFN_CONV):
        acc = acc + pltpu.roll(cat, s, axis=0)[SUBLANES:] * w[FFN_CONV - 1 - s:FFN_CONV - s, :]
    act_ref[...] = (_silu(acc) * u).astype(act_ref.dtype)
    last = a[a.shape[0] - SUBLANES:, :]
    prev_scr[...] = last

    @pl.when(i == n_row_tiles - 1)
    def _():
        tail_ref[...] = last


def ffn_up_conv(h, w_gate, w_up, layer, init, conv_w, *, tm_pref=1024, tn_pref=512):
    m, d = h.shape
    f = w_gate.shape[2]
    tm, tn = _tile(m, tm_pref), _tile(f, tn_pref)
    n_row_tiles = m // tm
    w_spec = pl.BlockSpec((None, d, tn), lambda j, i: (layer, 0, j))
    return pl.pallas_call(
        functools.partial(_ffn_up_conv_body, n_row_tiles=n_row_tiles),
        grid=(f // tn, n_row_tiles),
        in_specs=[pl.BlockSpec((tm, d), lambda j, i: (i, 0)), w_spec, w_spec,
                  pl.BlockSpec((SUBLANES, tn), lambda j, i: (0, j)),
                  pl.BlockSpec((FFN_CONV, tn), lambda j, i: (0, j))],
        out_specs=[pl.BlockSpec((tm, tn), lambda j, i: (i, j)),
                   pl.BlockSpec((SUBLANES, tn), lambda j, i: (0, j))],
        out_shape=[jax.ShapeDtypeStruct((m, f), bf16), jax.ShapeDtypeStruct((SUBLANES, f), f32)],
        scratch_shapes=[pltpu.VMEM((d, tn), bf16), pltpu.VMEM((d, tn), bf16),
                        pltpu.VMEM((SUBLANES, tn), f32)],
        compiler_params=_cparams(("arbitrary", "arbitrary")),
        name="ffn_up_conv",
    )(h, w_gate, w_up, init, conv_w)


def _rms_body(x_ref, g_ref, o_ref):
    x = x_ref[...]
    o_ref[...] = x * lax.rsqrt(jnp.mean(x * x, axis=-1, keepdims=True) + EPS) * g_ref[...]


def rms_norm(x, g):
    m, d = x.shape
    tm = _tile(m, 512)
    return pl.pallas_call(
        _rms_body,
        grid=(m // tm,),
        in_specs=[pl.BlockSpec((tm, d), lambda i: (i, 0)), pl.BlockSpec((1, d), lambda i: (0, 0))],
        out_specs=pl.BlockSpec((tm, d), lambda i: (i, 0)),
        out_shape=jax.ShapeDtypeStruct((m, d), f32),
        compiler_params=_cparams(("arbitrary",)),
        name="final_norm",
    )(x, g.reshape(1, d))


def _rope(x, cos, sin_signed):
    return x * cos + pltpu.roll(x, HEAD_DIM // 2, axis=1) * sin_signed


def _attn_prep_body(q_ref, k_ref, v_ref, cos_ref, sin_ref, qo, ko, vo):
    cos, sin = cos_ref[...], sin_ref[...]
    for h in range(N_ATT_HEADS):
        cols = slice(h * HEAD_DIM, (h + 1) * HEAD_DIM)
        qo[h] = _rope(q_ref[:, cols], cos, sin)
        ko[h] = _rope(k_ref[:, cols], cos, sin)
        vo[h] = v_ref[:, cols]


def attn_prep_sample(proj, cos, sin, n_seq):
    h_n = N_ATT_HEADS
    shp = jax.ShapeDtypeStruct((n_seq, h_n, SEQ_PAD, HEAD_DIM), f32)
    col = lambda cb: pl.BlockSpec((SEQ_PAD, ATT_W), lambda b: (b, cb))
    o_spec = pl.BlockSpec((None, h_n, SEQ_PAD, HEAD_DIM), lambda b: (b, 0, 0, 0))
    tab = pl.BlockSpec((SEQ_PAD, HEAD_DIM), lambda b: (0, 0))
    return pl.pallas_call(
        _attn_prep_body,
        grid=(n_seq,),
        in_specs=[col(0), col(1), col(2), tab, tab],
        out_specs=[o_spec, o_spec, o_spec],
        out_shape=[shp, shp, shp],
        compiler_params=_cparams(("arbitrary",)),
        name="attn_prep_sample",
    )(proj, proj, proj, cos, sin)


def _top_blocks(gate, n_valid, on_pick):
    lane = lax.broadcasted_iota(jnp.int32, gate.shape, 1)
    g = jnp.where(lane < n_valid, gate, -jnp.inf)
    for r in range(MOBA_TOPK):
        m = jnp.max(g, axis=1, keepdims=True)
        idx = jnp.min(jnp.where(g == m, lane, LANES), axis=1, keepdims=True)
        on_pick(r, idx, m > -jnp.inf)
        g = jnp.where(lane == idx, -jnp.inf, g)


def _attn_prep_t_body(q_ref, k_ref, v_ref, cos_ref, sin_ref, qt_o, ko, vo, kb_o, vt_o, kmo):
    cos, sin = cos_ref[...], sin_ref[...]
    for h in range(N_ATT_HEADS):
        cols = slice(h * HEAD_DIM, (h + 1) * HEAD_DIM)
        qt_o[h] = _rope(q_ref[:, cols], cos, sin).T
        kr = _rope(k_ref[:, cols], cos, sin)
        v = v_ref[:, cols]
        ko[h] = kr
        vo[h] = v
        kb_o[h] = kr.astype(bf16)
        vt_o[h] = v.T.astype(bf16)
        kmo[h] = jnp.mean(kr, axis=0, keepdims=True)


def attn_prep_prompt_t(proj, cos, sin):
    t = proj.shape[0]
    assert t % MOBA_BLOCK == 0
    nb = t // MOBA_BLOCK
    h_n = N_ATT_HEADS
    col = lambda cb: pl.BlockSpec((MOBA_BLOCK, ATT_W), lambda i: (i, cb))
    tab = pl.BlockSpec((MOBA_BLOCK, HEAD_DIM), lambda i: (i, 0))
    hm_spec = pl.BlockSpec((h_n, MOBA_BLOCK, HEAD_DIM), lambda i: (0, i, 0))
    t_spec = pl.BlockSpec((h_n, None, HEAD_DIM, MOBA_BLOCK), lambda i: (0, i, 0, 0))
    hm = lambda dt: jax.ShapeDtypeStruct((h_n, t, HEAD_DIM), dt)
    tr = lambda dt: jax.ShapeDtypeStruct((h_n, nb, HEAD_DIM, MOBA_BLOCK), dt)
    qt, k, v, kb, vt, km = pl.pallas_call(
        _attn_prep_t_body,
        grid=(nb,),
        in_specs=[col(0), col(1), col(2), tab, tab],
        out_specs=[t_spec, hm_spec, hm_spec, hm_spec, t_spec,
                   pl.BlockSpec((h_n, None, 1, HEAD_DIM), lambda i: (0, i, 0, 0))],
        out_shape=[tr(f32), hm(f32), hm(f32), hm(bf16), tr(bf16),
                   jax.ShapeDtypeStruct((h_n, nb, 1, HEAD_DIM), f32)],
        compiler_params=_cparams(("arbitrary",)),
        name="attn_prep_prompt",
    )(proj, proj, proj, cos, sin)
    return qt, k, v, kb, vt, km.reshape(h_n, nb, HEAD_DIM)


def _attn_prompt_t_body(qt_ref, kb_ref, vt_ref, km_ref, o_ref, m_scr, l_scr, acc_scr, *, n_blk):
    i = pl.program_id(1)
    blk = MOBA_BLOCK
    qt = qt_ref[...]
    gate = jnp.dot(km_ref[...], qt, precision=HIGHEST, preferred_element_type=f32)
    sub = lax.broadcasted_iota(jnp.int32, gate.shape, 0)
    g = jnp.where(sub < i, gate, -jnp.inf)
    sel = jnp.zeros(gate.shape, f32)
    for _ in range(MOBA_TOPK):
        m = jnp.max(g, axis=0, keepdims=True)
        idx = jnp.min(jnp.where(g == m, sub, LANES), axis=0, keepdims=True)
        pick = sub == idx
        sel = jnp.where(pick, jnp.where(m > -jnp.inf, 1.0, sel), sel)
        g = jnp.where(pick, -jnp.inf, g)
    bias = jnp.where(sel > 0.0, 0.0, NEG)
    q_aug = jnp.concatenate([(qt * (HEAD_DIM ** -0.5)).astype(bf16), bias.astype(bf16)], axis=0)

    m_scr[...] = jnp.full(m_scr.shape, NEG, f32)
    l_scr[...] = jnp.zeros_like(l_scr)
    acc_scr[...] = jnp.zeros_like(acc_scr)
    lane = lax.broadcasted_iota(jnp.int32, (blk, LANES), 1)

    def blocks(work):
        scores = []
        for st, j_blk, k_extra, mask in work:
            start = pl.multiple_of(j_blk * blk, blk)
            k_aug = jnp.concatenate([kb_ref[pl.ds(start, blk), :], k_extra], axis=1)
            s = jnp.dot(k_aug, q_aug, preferred_element_type=f32)
            scores.append(s if mask is None else jnp.where(mask, s, NEG))
        probs, alphas = [], []
        for (st, _, _, _), s in zip(work, scores):
            m_old = m_scr[st]
            m_new = jnp.maximum(m_old, jnp.max(s, axis=0, keepdims=True))
            alpha = jnp.exp(m_old - m_new)
            p = jnp.exp(s - m_new)
            l_scr[st] = alpha * l_scr[st] + jnp.sum(p, axis=0, keepdims=True)
            m_scr[st] = m_new
            probs.append(p.astype(bf16))
            alphas.append(alpha)
        pvs = [jnp.dot(vt_ref[j_blk], p, preferred_element_type=f32)
               for (_, j_blk, _, _), p in zip(work, probs)]
        for (st, _, _, _), alpha, pv in zip(work, alphas, pvs):
            acc_scr[st] = alpha * acc_scr[st] + pv

    def past_blocks(it, carry):
        work = []
        for st in range(ATTN_STREAMS):
            j = it * ATTN_STREAMS + st
            work.append((st, jnp.minimum(j, n_blk - 1), jnp.where(lane == j, 1.0, 0.0).astype(bf16), None))
        blocks(work)
        return carry

    lax.fori_loop(0, (i + ATTN_STREAMS - 1) // ATTN_STREAMS, past_blocks, 0)
    key = lax.broadcasted_iota(jnp.int32, (blk, blk), 0)
    qry = lax.broadcasted_iota(jnp.int32, (blk, blk), 1)
    blocks([(0, i, jnp.zeros((blk, LANES), bf16), key <= qry)])
    m_fin = m_scr[0]
    for st in range(1, ATTN_STREAMS):
        m_fin = jnp.maximum(m_fin, m_scr[st])
    l_fin = jnp.zeros((1, blk), f32)
    acc = jnp.zeros((HEAD_DIM, blk), f32)
    for st in range(ATTN_STREAMS):
        wgt = jnp.exp(m_scr[st] - m_fin)
        l_fin = l_fin + wgt * l_scr[st]
        acc = acc + wgt * acc_scr[st]
    o_ref[...] = (acc / l_fin).T.astype(o_ref.dtype)


def attn_prompt_t(qt, kb, vt, km):
    h_n, nb = qt.shape[:2]
    t = nb * MOBA_BLOCK
    assert nb + ATTN_STREAMS <= LANES and HEAD_DIM == LANES
    km_pad = jnp.pad(km, ((0, 0), (0, LANES - nb), (0, 0)))
    stat = pltpu.VMEM((ATTN_STREAMS, 1, MOBA_BLOCK), f32)
    return pl.pallas_call(
        functools.partial(_attn_prompt_t_body, n_blk=nb),
        grid=(h_n, nb),
        in_specs=[pl.BlockSpec((None, None, HEAD_DIM, MOBA_BLOCK), lambda h, i: (h, i, 0, 0)),
                  pl.BlockSpec((None, t, HEAD_DIM), lambda h, i: (h, 0, 0)),
                  pl.BlockSpec((None, nb, HEAD_DIM, MOBA_BLOCK), lambda h, i: (h, 0, 0, 0)),
                  pl.BlockSpec((None, LANES, HEAD_DIM), lambda h, i: (h, 0, 0))],
        out_specs=pl.BlockSpec((MOBA_BLOCK, HEAD_DIM), lambda h, i: (i, h)),
        out_shape=jax.ShapeDtypeStruct((t, h_n * HEAD_DIM), bf16),
        scratch_shapes=[stat, stat, pltpu.VMEM((ATTN_STREAMS, HEAD_DIM, MOBA_BLOCK), f32)],
        compiler_params=_cparams(("arbitrary", "arbitrary")),
        name="attn_prompt",
    )(qt, kb, vt, km_pad)


def _sample_select_body(pt_ref, *refs, n_blk, ppb, bps):
    n_pg = ppb * bps
    k_refs, q_ref, sel_ref, km_scr = refs[:n_pg], refs[n_pg], refs[n_pg + 1], refs[n_pg + 2]
    j = pl.program_id(1)

    @pl.when(j == 0)
    def _():
        km_scr[...] = jnp.zeros_like(km_scr)

    for bi in range(bps):
        ks = k_refs[bi * ppb][...].sum(axis=1)
        for p in range(1, ppb):
            ks = ks + k_refs[bi * ppb + p][...].sum(axis=1)
        ks = ks * (1.0 / MOBA_BLOCK)
        for h in range(N_ATT_HEADS):
            km_scr[h, pl.ds(j * bps + bi, 1), :] = ks[h:h + 1, :]

    @pl.when(j == n_blk // bps - 1)
    def _():
        lane = lax.broadcasted_iota(jnp.int32, (SEQ_PAD, LANES), 1)
        sel = [jnp.zeros((SEQ_PAD, LANES), jnp.int32)]
        for h in range(N_ATT_HEADS):
            gate = _dot_nt(q_ref[h], km_scr[h], HIGHEST)

            def on_pick(r, idx, has, h=h):
                sel[0] = jnp.where(lane == h * MOBA_TOPK + r, idx, sel[0])

            _top_blocks(gate, n_blk, on_pick)
        sel_ref[...] = sel[0]


def sample_select(cache_k_hm, layer, page_table, q_s):
    n_seq, n_pages = page_table.shape
    page = cache_k_hm.shape[3]
    ppb = MOBA_BLOCK // page
    assert ppb * page == MOBA_BLOCK and n_pages % ppb == 0
    n_blk = n_pages // ppb
    assert MOBA_TOPK <= n_blk <= LANES
    bps = 4 if n_blk % 4 == 0 else 1
    h_n = N_ATT_HEADS

    def page_spec(p):
        return pl.BlockSpec((None, None, h_n, page, HEAD_DIM),
                            lambda b, j, pt: (layer, pt[b, j * (bps * ppb) + p], 0, 0, 0))

    return pl.pallas_call(
        functools.partial(_sample_select_body, n_blk=n_blk, ppb=ppb, bps=bps),
        grid_spec=pltpu.PrefetchScalarGridSpec(
            num_scalar_prefetch=1, grid=(n_seq, n_blk // bps),
            in_specs=[page_spec(p) for p in range(bps * ppb)]
                     + [pl.BlockSpec((None, h_n, SEQ_PAD, HEAD_DIM), lambda b, j, pt: (b, 0, 0, 0))],
            out_specs=pl.BlockSpec((None, SEQ_PAD, LANES), lambda b, j, pt: (b, 0, 0)),
            scratch_shapes=[pltpu.VMEM((h_n, LANES, HEAD_DIM), f32)]),
        out_shape=jax.ShapeDtypeStruct((n_seq, SEQ_PAD, LANES), jnp.int32),
        compiler_params=_cparams(("arbitrary", "arbitrary")),
        name="sample_select",
    )(page_table, *([cache_k_hm] * (bps * ppb)), q_s)


def _sample_attn_body(pt_ref, sel_ref, ck_ref, cv_ref, q_ref, kn_ref, vn_ref, o_ref,
                      kbuf, vbuf, sem, *, layer, t_new, ppb, page, n_seq):
    b = pl.program_id(0)
    slot = b % 2
    h_n = N_ATT_HEADS
    heads = range(h_n)

    def copies(seq, sl, t):
        out = []
        for h in heads:
            th = t * h_n + h
            for r in range(MOBA_TOPK):
                blk = sel_ref[seq, th * MOBA_TOPK + r]
                for p in range(ppb):
                    pg = pt_ref[seq, blk * ppb + p]
                    dst = pl.ds((r * ppb + p) * page, page)
                    out.append(pltpu.make_async_copy(ck_ref.at[layer, pg, h], kbuf.at[sl, th, dst],
                                                     sem.at[sl, t, 0]))
                    out.append(pltpu.make_async_copy(cv_ref.at[layer, pg, h], vbuf.at[sl, th, dst],
                                                     sem.at[sl, t, 1]))
        return out

    def start_seq(seq, sl):
        for t in range(t_new):
            for cp in copies(seq, sl, t):
                cp.start()

    @pl.when(b == 0)
    def _():
        start_seq(b, slot)

    @pl.when(b + 1 < n_seq)
    def _():
        start_seq(b + 1, 1 - slot)

    o_ref[...] = jnp.zeros_like(o_ref)
    rown = lax.broadcasted_iota(jnp.int32, (SEQ_PAD, 1), 0)
    for t in range(t_new):
        for cp in copies(b, slot, t):
            cp.wait()
        qrow = [q_ref[h, t:t + 1, :] * (HEAD_DIM ** -0.5) for h in heads]
        s = [jnp.sum(kbuf[slot, t * h_n + h] * qrow[h], axis=1, keepdims=True) for h in heads]
        s_new = [jnp.where(rown <= t, jnp.sum(kn_ref[h] * qrow[h], axis=1, keepdims=True), NEG)
                 for h in heads]
        m = [jnp.maximum(jnp.max(s[h], axis=0, keepdims=True), jnp.max(s_new[h], axis=0, keepdims=True))
             for h in heads]
        e = [jnp.exp(s[h] - m[h]) for h in heads]
        e_new = [jnp.exp(s_new[h] - m[h]) for h in heads]
        den = [jnp.sum(e[h], axis=0, keepdims=True) + jnp.sum(e_new[h], axis=0, keepdims=True)
               for h in heads]
        num = [jnp.sum(e[h] * vbuf[slot, t * h_n + h], axis=0, keepdims=True)
               + jnp.sum(e_new[h] * vn_ref[h], axis=0, keepdims=True) for h in heads]
        for h in heads:
            o_ref[t:t + 1, h * HEAD_DIM:(h + 1) * HEAD_DIM] = num[h] / den[h]


def sample_attention(cache_k_hm, cache_v_hm, layer, page_table, sel, q_s, k_s, v_s, t_new):
    n_seq = page_table.shape[0]
    page = cache_k_hm.shape[3]
    ppb = MOBA_BLOCK // page
    h_n = N_ATT_HEADS
    n_sel = MOBA_TOPK * MOBA_BLOCK
    any_spec = pl.BlockSpec(memory_space=pl.ANY)
    seq_spec = pl.BlockSpec((None, h_n, SEQ_PAD, HEAD_DIM), lambda b, pt, sl: (b, 0, 0, 0))
    return pl.pallas_call(
        functools.partial(_sample_attn_body, layer=layer, t_new=t_new, ppb=ppb, page=page, n_seq=n_seq),
        grid_spec=pltpu.PrefetchScalarGridSpec(
            num_scalar_prefetch=2, grid=(n_seq,),
            in_specs=[any_spec, any_spec, seq_spec, seq_spec, seq_spec],
            out_specs=pl.BlockSpec((None, SEQ_PAD, h_n * HEAD_DIM), lambda b, pt, sl: (b, 0, 0)),
            scratch_shapes=[pltpu.VMEM((2, t_new * h_n, n_sel, HEAD_DIM), f32),
                            pltpu.VMEM((2, t_new * h_n, n_sel, HEAD_DIM), f32),
                            pltpu.SemaphoreType.DMA((2, t_new, 2))]),
        out_shape=jax.ShapeDtypeStruct((n_seq, SEQ_PAD, h_n * HEAD_DIM), f32),
        compiler_params=_cparams(("arbitrary",)),
        name="sample_attention",
    )(page_table, sel, cache_k_hm, cache_v_hm, q_s, k_s, v_s)


def _halo_cat(prev_ref, init_ref, x_ref):
    halo = jnp.where(pl.program_id(1) == 0, init_ref[...], prev_ref[...])
    return jnp.concatenate([halo, x_ref[...]], axis=0)


def _halo_specs(n_tiles, tm, halo, width, x_col, init_col):
    assert tm % halo == 0 or n_tiles == 1
    per = tm // halo

    def prev_map(b, i, *r):
        return (jnp.maximum((b * n_tiles + i) * per - 1, 0), x_col(*r))

    return [pl.BlockSpec((halo, width), prev_map),
            pl.BlockSpec((None, halo, width), lambda b, i, *r: (b, 0, init_col(*r))),
            pl.BlockSpec((tm, width), lambda b, i, *r: (b * n_tiles + i, x_col(*r)))]


def _gdn_prep_body(prev_ref, init_ref, x_ref, ab_ref, cw_ref, alog_ref, dtb_ref,
                   q_o, k_o, v_o, gb_o, *, tm, t_valid):
    cat = _halo_cat(prev_ref, init_ref, x_ref)
    w = cw_ref[...]
    acc = x_ref[...] * w[GDN_CONV - 1:GDN_CONV, :]
    for s in range(1, GDN_CONV):
        acc = acc + pltpu.roll(cat, s, axis=0)[SUBLANES:] * w[GDN_CONV - 1 - s:GDN_CONV - s, :]
    y = _silu(acc)
    row = pl.program_id(1) * tm + lax.broadcasted_iota(jnp.int32, (tm, 1), 0)
    valid = row < t_valid
    for h in range(N_GDN_HEADS):
        lo = h * HEAD_DIM
        qh = y[:, lo:lo + HEAD_DIM]
        kh = y[:, GDN_W + lo:GDN_W + lo + HEAD_DIM]
        qn = qh * lax.rsqrt(jnp.sum(qh * qh, axis=-1, keepdims=True) + EPS) * (HEAD_DIM ** -0.5)
        kn = kh * lax.rsqrt(jnp.sum(kh * kh, axis=-1, keepdims=True) + EPS)
        q_o[:, lo:lo + HEAD_DIM] = jnp.where(valid, qn, 0.0)
        k_o[:, lo:lo + HEAD_DIM] = jnp.where(valid, kn, 0.0)
    v_o[...] = jnp.where(valid, y[:, 2 * GDN_W:], 0.0)
    ab = ab_ref[...]
    g = -jnp.exp(alog_ref[...]) * _softplus(ab + dtb_ref[...])
    lane = lax.broadcasted_iota(jnp.int32, ab.shape, 1)
    gb = jnp.where(lane < N_GDN_HEADS, g, jnp.where(lane < 2 * N_GDN_HEADS, _sigmoid(ab), 0.0))
    gb_o[...] = jnp.where(valid, gb, 0.0)


def gdn_prep(proj, init, conv_w, a_log, dt_bias, *, n_seq, t_pad, t_valid, tm_pref=256):
    m = proj.shape[0]
    tm = _tile(t_pad, tm_pref)
    n_tiles = t_pad // tm
    w3 = 3 * GDN_W
    pad = lambda a: jnp.pad(a.reshape(1, -1), ((0, 0), (0, LANES - a.shape[-1])))
    ab_blk = COL_AB // LANES
    out = jax.ShapeDtypeStruct((m, GDN_W), f32)
    row_spec = lambda wd: pl.BlockSpec((tm, wd), lambda b, i: (b * n_tiles + i, 0))
    const = lambda r, c: pl.BlockSpec((r, c), lambda b, i: (0, 0))
    return pl.pallas_call(
        functools.partial(_gdn_prep_body, tm=tm, t_valid=t_valid),
        grid=(n_seq, n_tiles),
        in_specs=_halo_specs(n_tiles, tm, SUBLANES, w3, lambda: COL_QKV_B // w3, lambda: 0)
                 + [pl.BlockSpec((tm, LANES), lambda b, i: (b * n_tiles + i, ab_blk)),
                    const(GDN_CONV, w3), const(1, LANES), const(1, LANES)],
        out_specs=[row_spec(GDN_W), row_spec(GDN_W), row_spec(GDN_W), row_spec(LANES)],
        out_shape=[out, out, out, jax.ShapeDtypeStruct((m, LANES), f32)],
        compiler_params=_cparams(("arbitrary", "arbitrary")),
        name="gdn_prep",
    )(proj, init, proj, proj, conv_w, pad(a_log), pad(dt_bias))


def _gdn_intra_body(q_ref, k_ref, v_ref, gb_ref, qe_ref, o0_ref, m_ref, b_ref, gl_ref, *, rt):
    cs = GDN_CHUNK
    nc = rt // cs
    gb = gb_ref[...]
    r = lax.broadcasted_iota(jnp.int32, (rt, rt), 0)
    c = lax.broadcasted_iota(jnp.int32, (rt, rt), 1)
    in_chunk = c >= r - r % cs
    causal_f = jnp.where(c <= r, jnp.where(in_chunk, 1.0, 0.0), 0.0)
    strict_f = jnp.where(c < r, causal_f, 0.0)
    causal, strict = causal_f > 0.0, strict_f > 0.0
    eye = jnp.where(r == c, 1.0, 0.0)
    gcum = jnp.dot(causal_f, gb, precision=HIGHEST, preferred_element_type=f32)
    gl_rows = jnp.concatenate(
        [jnp.broadcast_to(gcum[(j + 1) * cs - 1:(j + 1) * cs, :], (cs, LANES)) for j in range(nc)], axis=0)
    for j in range(nc):
        gl_ref[j] = jnp.exp(gl_rows[j * cs:j * cs + SUBLANES, :])
    heads = range(N_GDN_HEADS)
    cols = [slice(h * HEAD_DIM, (h + 1) * HEAD_DIM) for h in heads]
    q, k, v = ([ref[:, cl] for cl in cols] for ref in (q_ref, k_ref, v_ref))
    gcol = [gcum[:, h:h + 1] for h in heads]
    bcol = [gb[:, N_GDN_HEADS + h:N_GDN_HEADS + h + 1] for h in heads]

    def decay_of(gc):
        gi = jnp.broadcast_to(gc, (rt, rt))
        grow = jnp.sum(jnp.where(r == c, gi, 0.0), axis=0, keepdims=True)
        return jnp.exp(jnp.where(causal, gi - grow, NEG))

    decay = [decay_of(gcol[h]) for h in heads]
    kb = [k[h].astype(bf16) for h in heads]
    a = [jnp.where(strict, bcol[h] * _dot_nt(kb[h], kb[h]) * decay[h], 0.0) for h in heads]
    x = [eye - a[h] for h in heads]
    p = [_dot3(a[h], a[h]) for h in heads]
    n_sq = max(1, (cs - 1).bit_length() - 1)
    for it in range(n_sq):
        x = [x[h] + _dot3(x[h], p[h]) for h in heads]
        if it + 1 < n_sq:
            p = [_dot3(p[h], p[h]) for h in heads]
    eg = [jnp.exp(gcol[h]) for h in heads]
    sol = [_dot3(x[h], jnp.concatenate([v[h] * bcol[h], k[h] * (bcol[h] * eg[h])], axis=1))
           for h in heads]
    qk = [jnp.where(causal, _dot_nt(q[h].astype(bf16), kb[h]) * decay[h], 0.0) for h in heads]
    qs = [_bdot(qk[h], sol[h]) for h in heads]
    for h in heads:
        o0_ref[:, cols[h]] = qs[h][:, :HEAD_DIM]
        qe_ref[:, cols[h]] = q[h] * eg[h] - qs[h][:, HEAD_DIM:]
    for h in heads:
        kt = (k[h] * jnp.exp(gl_rows[:, h:h + 1] - gcol[h])).astype(bf16)
        solb = sol[h].astype(bf16)
        for j in range(nc):
            rows = slice(j * cs, (j + 1) * cs)
            mb = _dot_tn(kt[rows], solb[rows])
            b_ref[j, h] = mb[:, :HEAD_DIM]
            m_ref[j, h] = -mb[:, HEAD_DIM:]


def _gdn_scan_body(qe_ref, o0_ref, m_ref, b_ref, gl_ref, z_ref, nw_ref, s0_ref, o_ref, s_out_ref, s_scr,
                   *, cps, n_steps):
    step = pl.program_id(1)
    cs = GDN_CHUNK

    @pl.when(step == 0)
    def _():
        s_scr[...] = s0_ref[...]

    for j in range(cps):
        rows = slice(j * cs, (j + 1) * cs)
        for h in range(N_GDN_HEADS):
            cols = slice(h * HEAD_DIM, (h + 1) * HEAD_DIM)
            s = s_scr[h]
            sb = s.astype(bf16)
            o = jnp.dot(qe_ref[rows, cols].astype(bf16), sb, preferred_element_type=f32) + o0_ref[rows, cols]
            s_scr[h] = (s * gl_ref[j, 0:1, h:h + 1]
                        + jnp.dot(m_ref[j, h].astype(bf16), sb, preferred_element_type=f32) + b_ref[j, h])
            on = o * lax.rsqrt(jnp.mean(o * o, axis=-1, keepdims=True) + EPS) * nw_ref[...]
            o_ref[rows, cols] = (on * _silu(z_ref[rows, cols])).astype(o_ref.dtype)

    @pl.when(step == n_steps - 1)
    def _():
        s_out_ref[...] = s_scr[...]


def gdn_recurrence(q, k, v, gb, z, z_blk, norm_w, s0, s0_layer, *, n_seq, t_pad):
    m = q.shape[0]
    cs = GDN_CHUNK
    assert t_pad % cs == 0
    rt = _tile(m, GDN_TILE)
    assert rt % cs == 0
    nct = rt // cs
    nc_all = m // cs
    h_n = N_GDN_HEADS
    rows = lambda wd: pl.BlockSpec((rt, wd), lambda i: (i, 0))
    mats = pl.BlockSpec((nct, h_n, HEAD_DIM, HEAD_DIM), lambda i: (i, 0, 0, 0))
    mat_shape = jax.ShapeDtypeStruct((nc_all, h_n, HEAD_DIM, HEAD_DIM), f32)
    qe, o0, mm, bb, gl = pl.pallas_call(
        functools.partial(_gdn_intra_body, rt=rt),
        grid=(m // rt,),
        in_specs=[rows(GDN_W), rows(GDN_W), rows(GDN_W), rows(LANES)],
        out_specs=[rows(GDN_W), rows(GDN_W), mats, mats,
                   pl.BlockSpec((nct, SUBLANES, LANES), lambda i: (i, 0, 0))],
        out_shape=[jax.ShapeDtypeStruct((m, GDN_W), f32), jax.ShapeDtypeStruct((m, GDN_W), f32),
                   mat_shape, mat_shape, jax.ShapeDtypeStruct((nc_all, SUBLANES, LANES), f32)],
        compiler_params=_cparams(("arbitrary",)),
        name="gdn_intra",
    )(q, k, v, gb)

    n_chunks = t_pad // cs
    cps = next(c for c in (4, 2, 1) if n_chunks % c == 0)
    n_steps = n_chunks // cps
    srow = lambda wd, cb=0: pl.BlockSpec((cps * cs, wd), lambda b, s: (b * n_steps + s, cb))
    smat = pl.BlockSpec((cps, h_n, HEAD_DIM, HEAD_DIM), lambda b, s: (b * n_steps + s, 0, 0, 0))
    return pl.pallas_call(
        functools.partial(_gdn_scan_body, cps=cps, n_steps=n_steps),
        grid=(n_seq, n_steps),
        in_specs=[srow(GDN_W), srow(GDN_W), smat, smat,
                  pl.BlockSpec((cps, SUBLANES, LANES), lambda b, s: (b * n_steps + s, 0, 0)),
                  srow(GDN_W, z_blk), pl.BlockSpec((1, HEAD_DIM), lambda b, s: (0, 0)),
                  pl.BlockSpec((None, None, h_n, HEAD_DIM, HEAD_DIM), lambda b, s: (s0_layer, b, 0, 0, 0))],
        out_specs=[srow(GDN_W),
                   pl.BlockSpec((None, h_n, HEAD_DIM, HEAD_DIM), lambda b, s: (b, 0, 0, 0))],
        out_shape=[jax.ShapeDtypeStruct((m, GDN_W), bf16),
                   jax.ShapeDtypeStruct((n_seq, h_n, HEAD_DIM, HEAD_DIM), f32)],
        scratch_shapes=[pltpu.VMEM((h_n, HEAD_DIM, HEAD_DIM), f32)],
        compiler_params=_cparams(("arbitrary", "arbitrary")),
        name="gdn_scan",
    )(qe, o0, mm, bb, gl, z, norm_w.reshape(1, HEAD_DIM), s0)


def _pool_body(prev_ref, init_ref, x_ref, pw_ref, ps_ref, o_ref, *, tm, start):
    cat = _halo_cat(prev_ref, init_ref, x_ref)
    sums, s, span = [], cat, 1
    for _ in POOL_WINDOWS:
        s = s + pltpu.roll(s, span, axis=0)
        span *= 2
        sums.append(s)
    assert span == POOL_HALO
    pos = start + pl.program_id(1) * tm + lax.broadcasted_iota(jnp.int32, (tm, 1), 0)
    x = x_ref[...]
    for g, win in enumerate(POOL_WINDOWS):
        lo = g * POOL_GROUP_W
        cnt = jnp.minimum(win, pos + 1).astype(f32)
        mix = sums[g][POOL_HALO:, lo:lo + POOL_GROUP_W] / cnt - x[:, lo:lo + POOL_GROUP_W]
        out = jnp.dot(mix, pw_ref[g], precision=HIGHEST, preferred_element_type=f32)
        o_ref[:, lo:lo + POOL_GROUP_W] = (out * ps_ref[:, lo:lo + POOL_GROUP_W]).astype(o_ref.dtype)


def pool_mix(proj, init, pool_w, pool_scale, *, n_seq, t_pad, start, tm_pref=512):
    m = proj.shape[0]
    tm = _tile(t_pad, tm_pref)
    n_tiles = t_pad // tm
    return pl.pallas_call(
        functools.partial(_pool_body, tm=tm, start=start),
        grid=(n_seq, n_tiles),
        in_specs=_halo_specs(n_tiles, tm, POOL_HALO, POOL_W, lambda: COL_POOL // POOL_W, lambda: 0)
                 + [pl.BlockSpec(pool_w.shape, lambda b, i: (0, 0, 0)),
                    pl.BlockSpec((1, POOL_W), lambda b, i: (0, 0))],
        out_specs=pl.BlockSpec((tm, POOL_W), lambda b, i: (b * n_tiles + i, 0)),
        out_shape=jax.ShapeDtypeStruct((m, POOL_W), _act_dtype(tm)),
        compiler_params=_cparams(("arbitrary", "arbitrary")),
        name="pool_mix",
    )(proj, init, proj, pool_w, pool_scale.reshape(1, POOL_W))


def _conv_act_body(prev_ref, init_ref, a_ref, u_ref, cw_ref, o_ref):
    cat = _halo_cat(prev_ref, init_ref, a_ref)
    w = cw_ref[...]
    acc = a_ref[...] * w[FFN_CONV - 1:FFN_CONV, :]
    for s in range(1, FFN_CONV):
        acc = acc + pltpu.roll(cat, s, axis=0)[SUBLANES:] * w[FFN_CONV - 1 - s:FFN_CONV - s, :]
    o_ref[...] = (_silu(acc) * u_ref[...]).astype(o_ref.dtype)


def conv_act(a, u, init, conv_w, *, n_seq, t_pad, tm_pref=512, tile_elems=512 * 512):
    m, f = a.shape
    tm = _tile(t_pad, tm_pref)
    tf = _tile(f, max(LANES, tile_elems // tm // LANES * LANES))
    n_tiles = t_pad // tm
    return pl.pallas_call(
        _conv_act_body,
        grid=(n_seq, n_tiles, f // tf),
        in_specs=_halo_specs(n_tiles, tm, SUBLANES, tf, lambda j: j, lambda j: j)
                 + [pl.BlockSpec((tm, tf), lambda b, i, j: (b * n_tiles + i, j)),
                    pl.BlockSpec((FFN_CONV, tf), lambda b, i, j: (0, j))],
        out_specs=pl.BlockSpec((tm, tf), lambda b, i, j: (b * n_tiles + i, j)),
        out_shape=jax.ShapeDtypeStruct((m, f), _act_dtype(tm)),
        compiler_params=_cparams(("arbitrary", "arbitrary", "arbitrary")),
        name="conv_act",
    )(a, init, a, u, conv_w)


def _rope_tables(pos):
    half = HEAD_DIM // 2
    inv = ROPE_THETA ** (-jnp.arange(half, dtype=f32) * 2.0 / HEAD_DIM)
    ang = pos.astype(f32)[:, None] * inv[None, :]
    cos, sin = jnp.cos(ang), jnp.sin(ang)
    return jnp.concatenate([cos, cos], axis=-1), jnp.concatenate([-sin, sin], axis=-1)


def _reorder_w_in(w_in):
    qkv = 3 * ATT_W + 3 * GDN_W
    z_lo = qkv
    ab_lo = z_lo + GDN_W
    pool_lo = ab_lo + 2 * N_GDN_HEADS
    assert w_in.shape[-1] == pool_lo + POOL_W
    zeros = jnp.zeros(w_in.shape[:-1] + (COL_Z - COL_AB - 2 * N_GDN_HEADS,), w_in.dtype)
    w = jnp.concatenate([w_in[..., :qkv], w_in[..., pool_lo:], w_in[..., ab_lo:pool_lo], zeros,
                         w_in[..., z_lo:ab_lo]], axis=-1)
    return w.astype(bf16)


def _tail_rows(buf, x, t_valid, lo, hi, n):
    if t_valid >= n:
        return x[:, t_valid - n:t_valid, lo:hi]
    return jnp.concatenate([buf[:, t_valid:], x[:, :t_valid, lo:hi]], axis=1)


def _front_pad(state, rows):
    return jnp.pad(state, ((0, 0), (rows - state.shape[1], 0), (0, 0)))


def _group_layer(x, mods, layer, lw, *, n_seq, t_pad, t_valid, start, states, attn_fn):
    (g_mix, g_ffn, w_in_r, gdn_conv_w, gdn_a_log, gdn_dt_bias, gdn_norm_w, pool_w, pool_scale,
     w_out, w_gate, w_up, ffn_conv_w, w_down) = lw
    sh1, sc1, gt1, sh2, sc2, gt2 = mods
    s_gdn, s_gdn_layer, gdn_buf, pool_buf, ffn_buf = states
    m = x.shape[0]
    (proj,) = matmul(norm_mod(x, g_mix, sc1, sh1), [w_in_r], layer, name="in_proj")
    proj3 = proj.reshape(n_seq, t_pad, IN_W_PAD)

    o_a, k_new, v_new = attn_fn(proj)

    qn, kn, vc, gb = gdn_prep(proj, _front_pad(gdn_buf, SUBLANES), gdn_conv_w, gdn_a_log, gdn_dt_bias,
                              n_seq=n_seq, t_pad=t_pad, t_valid=t_valid)
    t_gdn = -(-t_pad // GDN_CHUNK) * GDN_CHUNK
    if t_gdn == t_pad:
        o_b, s_new = gdn_recurrence(qn, kn, vc, gb, proj, COL_Z // GDN_W, gdn_norm_w, s_gdn, s_gdn_layer,
                                    n_seq=n_seq, t_pad=t_pad)
    else:
        def chunk_pad(a):
            a = a.reshape(n_seq, t_pad, a.shape[-1])
            return jnp.pad(a, ((0, 0), (0, t_gdn - t_pad), (0, 0))).reshape(n_seq * t_gdn, -1)
        o_b, s_new = gdn_recurrence(chunk_pad(qn), chunk_pad(kn), chunk_pad(vc), chunk_pad(gb),
                                    chunk_pad(proj[:, COL_Z:COL_Z + GDN_W]), 0, gdn_norm_w, s_gdn,
                                    s_gdn_layer, n_seq=n_seq, t_pad=t_gdn)
        o_b = o_b.reshape(n_seq, t_gdn, GDN_W)[:, :t_pad].reshape(m, GDN_W)
    new_gdn_buf = _tail_rows(gdn_buf, proj3, t_valid, COL_QKV_B, COL_QKV_B + 3 * GDN_W, GDN_CONV - 1)

    o_p = pool_mix(proj, _front_pad(pool_buf, POOL_HALO), pool_w, pool_scale,
                   n_seq=n_seq, t_pad=t_pad, start=start)
    new_pool_buf = _tail_rows(pool_buf, proj3, t_valid, COL_POOL, COL_POOL + POOL_W, POOL_BUF)

    mixed = jnp.concatenate([o.astype(bf16) for o in (o_a, o_b, o_p)], axis=1)
    (x,) = matmul(mixed, [w_out], layer, x=x, gt=gt1, name="out_proj")

    h2 = norm_mod(x, g_ffn, sc2, sh2)
    init_a = _front_pad(ffn_buf, SUBLANES)
    if n_seq == 1 and t_valid == t_pad and t_pad >= SUBLANES:
        act, a_tail = ffn_up_conv(h2, w_gate, w_up, layer, init_a[0], ffn_conv_w)
        new_ffn_buf = a_tail[None, SUBLANES - (FFN_CONV - 1):]
    else:
        a, u = matmul(h2, [w_gate, w_up], layer, name="ffn_up")
        act = conv_act(a, u, init_a, ffn_conv_w, n_seq=n_seq, t_pad=t_pad)
        f = a.shape[1]
        new_ffn_buf = _tail_rows(ffn_buf, a.reshape(n_seq, t_pad, f), t_valid, 0, f, FFN_CONV - 1)
    (x,) = matmul(act, [w_down], layer, x=x, gt=gt2, tm_pref=512, name="ffn_down")
    return x, k_new, v_new, s_new, new_gdn_buf, new_pool_buf, new_ffn_buf


def kernel(x_prompt, x_sample, cache_k, cache_v, page_table, state_gdn, state_gdn_conv, state_pool,
           state_ffn_conv, c_prompt, c_sample, w_ada, b_ada, g_mix, g_ffn, w_in, gdn_conv_w, gdn_a_log,
           gdn_dt_bias, gdn_norm_w, pool_w, pool_scale, w_out, w_gate, w_up, ffn_conv_w, w_down, g_final):
    bp, t_p, d = x_prompt.shape
    bs, t_s, _ = x_sample.shape
    depth = w_ada.shape[0]
    page = cache_k.shape[2]
    past_len = page_table.shape[1] * page
    assert bp == 1 and t_s <= SEQ_PAD and t_p % MOBA_BLOCK == 0 and past_len % MOBA_BLOCK == 0
    assert t_p >= POOL_BUF

    ck = jnp.transpose(cache_k, (0, 1, 3, 2, 4))
    cv = jnp.transpose(cache_v, (0, 1, 3, 2, 4))

    n_c = bp + bs
    mc = -(-n_c // SUBLANES) * SUBLANES
    c_all = jnp.pad(jnp.concatenate([c_prompt, c_sample], axis=0), ((0, mc - n_c), (0, 0)))
    mod = ada_mod(c_all, w_ada, b_ada)
    w_in_r = _reorder_w_in(w_in)

    cos_p, sin_p = _rope_tables(jnp.arange(t_p))
    cos_s, sin_s = _rope_tables(past_len + jnp.arange(SEQ_PAD))

    xp = x_prompt.reshape(bp * t_p, d)
    xs = jnp.pad(x_sample, ((0, 0), (0, SEQ_PAD - t_s), (0, 0))).reshape(bs * SEQ_PAD, d)

    zeros_p = lambda *shape: jnp.zeros((bp,) + shape, f32)
    outs_p, outs_s = [], []
    for l in range(depth):
        lw = (g_mix[l], g_ffn[l], w_in_r, gdn_conv_w[l], gdn_a_log[l], gdn_dt_bias[l], gdn_norm_w[l],
              pool_w[l], pool_scale[l], w_out, w_gate, w_up, ffn_conv_w[l], w_down)
        mods_p = [mod[l, 0:bp, i * d:(i + 1) * d] for i in range(6)]
        mods_s = [jnp.repeat(mod[l, bp:n_c, i * d:(i + 1) * d], SEQ_PAD, axis=0) for i in range(6)]

        def attn_p(proj):
            qt, k, v, kb, vt, km = attn_prep_prompt_t(proj, cos_p, sin_p)
            return attn_prompt_t(qt, kb, vt, km), k, v

        def attn_s(proj, l=l):
            q, k, v = attn_prep_sample(proj, cos_s, sin_s, bs)
            sel = sample_select(ck, l, page_table, q)
            sel = sel[:, :t_s, :N_ATT_HEADS * MOBA_TOPK].reshape(bs, -1)
            o = sample_attention(ck, cv, l, page_table, sel, q, k, v, t_s)
            return o.reshape(bs * SEQ_PAD, ATT_W), k[:, :, :t_s], v[:, :, :t_s]

        xp, *st_p = _group_layer(
            xp, mods_p, l, lw, n_seq=bp, t_pad=t_p, t_valid=t_p, start=0, attn_fn=attn_p,
            states=(zeros_p(N_GDN_HEADS, HEAD_DIM, HEAD_DIM)[None], 0, zeros_p(GDN_CONV - 1, 3 * GDN_W),
                    zeros_p(POOL_BUF, POOL_W), zeros_p(FFN_CONV - 1, w_gate.shape[-1])))
        xs, *st_s = _group_layer(
            xs, mods_s, l, lw, n_seq=bs, t_pad=SEQ_PAD, t_valid=t_s, start=past_len, attn_fn=attn_s,
            states=(state_gdn, l, state_gdn_conv[l], state_pool[l], state_ffn_conv[l]))
        outs_p.append(st_p)
        outs_s.append(st_s)

    y_prompt = rms_norm(xp, g_final).reshape(bp, t_p, d)
    y_sample = rms_norm(xs, g_final).reshape(bs, SEQ_PAD, d)[:, :t_s]
    kp, vp, gp, gcp, pp, fp = [jnp.stack(t) for t in zip(*outs_p)]
    ks, vs, gs, gcs, ps, fs = [jnp.stack(t) for t in zip(*outs_s)]
    k_prompt = jnp.transpose(kp, (0, 2, 1, 3))[:, None]
    v_prompt = jnp.transpose(vp, (0, 2, 1, 3))[:, None]
    k_sample = jnp.transpose(ks, (0, 1, 3, 2, 4))
    v_sample = jnp.transpose(vs, (0, 1, 3, 2, 4))
    return (y_prompt, y_sample, k_prompt, v_prompt, k_sample, v_sample, gp, gs, gcp, gcs, pp, ps, fp, fs)
```

```python
import functools

import jax
import jax.numpy as jnp
from jax import lax
from jax.experimental import pallas as pl
from jax.experimental.pallas import tpu as pltpu

f32 = jnp.float32
bf16 = jnp.bfloat16
HIGHEST = lax.Precision.HIGHEST

LANES = 128
SUBLANES = 8
VMEM_LIMIT_BYTES = 56 * 1024 * 1024

HEAD_DIM = 128
N_ATT_HEADS = 6
N_GDN_HEADS = 6
ATT_W = N_ATT_HEADS * HEAD_DIM
GDN_W = N_GDN_HEADS * HEAD_DIM
POOL_WINDOWS = (2, 4, 8, 16)
POOL_GROUP_W = 128
POOL_W = len(POOL_WINDOWS) * POOL_GROUP_W
POOL_BUF = 15
POOL_HALO = 16
MOBA_BLOCK = 256
MOBA_TOPK = 3
ATTN_STREAMS = 8
V_ROWS = 128 + 16
GDN_CHUNK = 64
GDN_BASE = 16
GDN_TILE = 128
GDN_CONV = 4
FFN_CONV = 3
ROPE_THETA = 10000.0
EPS = 1e-6
SEQ_PAD = 8
NEG = -1e30
LOG2_E = 1.4426950408889634

COL_QKV_A = 0
COL_QKV_B = 3 * ATT_W
COL_POOL = COL_QKV_B + 3 * GDN_W
COL_AB = COL_POOL + POOL_W
COL_Z = COL_AB + 256
IN_W_PAD = COL_Z + GDN_W


def _cparams(sem):
    return pltpu.CompilerParams(dimension_semantics=sem, vmem_limit_bytes=VMEM_LIMIT_BYTES)


def _tile(n, pref):
    t = min(n, pref)
    while n % t:
        t -= SUBLANES
    assert t > 0 and n % t == 0
    return t


def _act_dtype(tm):
    return bf16 if tm % (2 * SUBLANES) == 0 else f32


def _sigmoid(x):
    return 1.0 / (1.0 + jnp.exp(-x))


def _silu(x):
    return x * _sigmoid(x)


def _softplus(x):
    return jnp.maximum(x, 0.0) + jnp.log(1.0 + jnp.exp(-jnp.abs(x)))


def _dot_nt(a, b, precision=None):
    return lax.dot_general(a, b, (((1,), (1,)), ((), ())), precision=precision,
                           preferred_element_type=f32)


def _dot_tn(a, b, precision=None):
    return lax.dot_general(a, b, (((0,), (0,)), ((), ())), precision=precision,
                           preferred_element_type=f32)


def _bdot(a, b):
    return jnp.dot(a.astype(bf16), b.astype(bf16), preferred_element_type=f32)


def _split_bf16(a):
    hi = a.astype(bf16)
    return hi, (a - hi.astype(f32)).astype(bf16)


def _dot3(a, b):
    (ah, al), (bh, bl) = _split_bf16(a), _split_bf16(b)
    d = functools.partial(jnp.dot, preferred_element_type=f32)
    return d(ah, bh) + (d(ah, bl) + d(al, bh))


def _mod_spec(mod, tm, ncols):
    if mod.shape[0] == 1:
        return pl.BlockSpec((1, ncols), lambda i, *_: (0, 0))
    return pl.BlockSpec((tm, ncols), lambda i, *_: (i, 0))


def _ada_body(c_ref, w_ref, b_ref, o_ref):
    s = _silu(c_ref[...])
    o_ref[...] = jnp.dot(s.astype(bf16), w_ref[...].astype(bf16),
                         preferred_element_type=f32) + b_ref[...]


def ada_mod(c_all, w_ada, b_ada):
    mc, d = c_all.shape
    depth, _, n = w_ada.shape
    tn = _tile(n, 1024)
    return pl.pallas_call(
        _ada_body,
        grid=(depth, n // tn),
        in_specs=[pl.BlockSpec((mc, d), lambda l, j: (0, 0)),
                  pl.BlockSpec((None, d, tn), lambda l, j: (l, 0, j)),
                  pl.BlockSpec((None, 1, tn), lambda l, j: (l, 0, j))],
        out_specs=pl.BlockSpec((None, mc, tn), lambda l, j: (l, 0, j)),
        out_shape=jax.ShapeDtypeStruct((depth, mc, n), f32),
        compiler_params=_cparams(("arbitrary", "arbitrary")),
        name="ada_mod",
    )(c_all, w_ada, b_ada.reshape(depth, 1, n))


def _norm_mod_body(x_ref, g_ref, sc_ref, sh_ref, o_ref):
    x = x_ref[...]
    y = x * lax.rsqrt(jnp.mean(x * x, axis=-1, keepdims=True) + EPS)
    o_ref[...] = ((y * g_ref[...]) * (1.0 + sc_ref[...]) + sh_ref[...]).astype(o_ref.dtype)


def norm_mod(x, g, sc, sh, *, tm_pref=512, name="norm_mod"):
    m, d = x.shape
    tm = _tile(m, tm_pref)
    return pl.pallas_call(
        _norm_mod_body,
        grid=(m // tm,),
        in_specs=[pl.BlockSpec((tm, d), lambda i: (i, 0)), pl.BlockSpec((1, d), lambda i: (0, 0)),
                  _mod_spec(sc, tm, d), _mod_spec(sh, tm, d)],
        out_specs=pl.BlockSpec((tm, d), lambda i: (i, 0)),
        out_shape=jax.ShapeDtypeStruct((m, d), bf16),
        compiler_params=_cparams(("arbitrary",)),
        name=name,
    )(x, g.reshape(1, d), sc, sh)


def _mm_body(a_ref, *rest, n_w, residual, cast_w):
    w_refs, rest = rest[:n_w], rest[n_w:]
    if residual:
        (x_ref, gt_ref), rest = rest[:2], rest[2:]
    o_refs, w_scrs = rest[:n_w], rest[n_w:]

    if cast_w:
        @pl.when(pl.program_id(1) == 0)
        def _():
            for w_ref, w_scr in zip(w_refs, w_scrs):
                w_scr[...] = w_ref[...].astype(bf16)

    a = a_ref[...].astype(bf16)
    for n_i, o_ref in enumerate(o_refs):
        w = w_scrs[n_i][...] if cast_w else w_refs[n_i][...]
        acc = jnp.dot(a, w, preferred_element_type=f32)
        o_ref[...] = x_ref[...] + gt_ref[...] * acc if residual else acc


def matmul(a, ws, layer, *, x=None, gt=None, tm_pref=1024, tn_pref=512, name="mm"):
    m, kdim = a.shape
    n = ws[0].shape[2]
    tm, tn = _tile(m, tm_pref), _tile(n, tn_pref)
    n_w = len(ws)
    residual = x is not None
    assert not residual or n_w == 1
    cast_w = ws[0].dtype != bf16
    in_specs = [pl.BlockSpec((tm, kdim), lambda j, i: (i, 0))]
    in_specs += [pl.BlockSpec((None, kdim, tn), lambda j, i: (layer, 0, j)) for _ in ws]
    args = [a, *ws]
    if residual:
        gt_spec = (pl.BlockSpec((1, tn), lambda j, i: (0, j)) if gt.shape[0] == 1
                   else pl.BlockSpec((tm, tn), lambda j, i: (i, j)))
        in_specs += [pl.BlockSpec((tm, tn), lambda j, i: (i, j)), gt_spec]
        args += [x, gt]
    outs = pl.pallas_call(
        functools.partial(_mm_body, n_w=n_w, residual=residual, cast_w=cast_w),
        grid=(n // tn, m // tm),
        in_specs=in_specs,
        out_specs=[pl.BlockSpec((tm, tn), lambda j, i: (i, j)) for _ in ws],
        out_shape=[jax.ShapeDtypeStruct((m, n), f32) for _ in ws],
        scratch_shapes=[pltpu.VMEM((kdim, tn), bf16) for _ in ws] if cast_w else [],
        compiler_params=_cparams(("arbitrary", "arbitrary")),
        name=name,
    )(*args)
    return outs


def _ffn_up_conv_body(h_ref, wg_ref, wu_ref, init_ref, cw_ref, act_ref, tail_ref,
                      wg_scr, wu_scr, prev_scr, *, n_row_tiles):
    i = pl.program_id(1)

    @pl.when(i == 0)
    def _():
        wg_scr[...] = wg_ref[...].astype(bf16)
        wu_scr[...] = wu_ref[...].astype(bf16)
        prev_scr[...] = init_ref[...]

    h = h_ref[...]
    a = jnp.dot(h, wg_scr[...], preferred_element_type=f32)
    u = jnp.dot(h, wu_scr[...], preferred_element_type=f32)
    cat = jnp.concatenate([prev_scr[...], a], axis=0)
    w = cw_ref[...]
    acc = a * w[FFN_CONV - 1:FFN_CONV, :]
    for s in range(1, FFN_CONV):
        acc = acc + pltpu.roll(cat, s, axis=0)[SUBLANES:] * w[FFN_CONV - 1 - s:FFN_CONV - s, :]
    act_ref[...] = (_silu(acc) * u).astype(act_ref.dtype)
    last = a[a.shape[0] - SUBLANES:, :]
    prev_scr[...] = last

    @pl.when(i == n_row_tiles - 1)
    def _():
        tail_ref[...] = last


def ffn_up_conv(h, w_gate, w_up, layer, init, conv_w, *, tm_pref=1024, tn_pref=512):
    m, d = h.shape
    f = w_gate.shape[2]
    tm, tn = _tile(m, tm_pref), _tile(f, tn_pref)
    n_row_tiles = m // tm
    w_spec = pl.BlockSpec((None, d, tn), lambda j, i: (layer, 0, j))
    return pl.pallas_call(
        functools.partial(_ffn_up_conv_body, n_row_tiles=n_row_tiles),
        grid=(f // tn, n_row_tiles),
        in_specs=[pl.BlockSpec((tm, d), lambda j, i: (i, 0)), w_spec, w_spec,
                  pl.BlockSpec((SUBLANES, tn), lambda j, i: (0, j)),
                  pl.BlockSpec((FFN_CONV, tn), lambda j, i: (0, j))],
        out_specs=[pl.BlockSpec((tm, tn), lambda j, i: (i, j)),
                   pl.BlockSpec((SUBLANES, tn), lambda j, i: (0, j))],
        out_shape=[jax.ShapeDtypeStruct((m, f), bf16), jax.ShapeDtypeStruct((SUBLANES, f), f32)],
        scratch_shapes=[pltpu.VMEM((d, tn), bf16), pltpu.VMEM((d, tn), bf16),
                        pltpu.VMEM((SUBLANES, tn), f32)],
        compiler_params=_cparams(("arbitrary", "arbitrary")),
        name="ffn_up_conv",
    )(h, w_gate, w_up, init, conv_w)


def _rms_body(x_ref, g_ref, o_ref):
    x = x_ref[...]
    o_ref[...] = x * lax.rsqrt(jnp.mean(x * x, axis=-1, keepdims=True) + EPS) * g_ref[...]


def rms_norm(x, g):
    m, d = x.shape
    tm = _tile(m, 512)
    return pl.pallas_call(
        _rms_body,
        grid=(m // tm,),
        in_specs=[pl.BlockSpec((tm, d), lambda i: (i, 0)), pl.BlockSpec((1, d), lambda i: (0, 0))],
        out_specs=pl.BlockSpec((tm, d), lambda i: (i, 0)),
        out_shape=jax.ShapeDtypeStruct((m, d), f32),
        compiler_params=_cparams(("arbitrary",)),
        name="final_norm",
    )(x, g.reshape(1, d))


def _rope(x, cos, sin_signed):
    return x * cos + pltpu.roll(x, HEAD_DIM // 2, axis=1) * sin_signed


def _attn_prep_body(q_ref, k_ref, v_ref, cos_ref, sin_ref, qo, ko, vo):
    cos, sin = cos_ref[...], sin_ref[...]
    for h in range(N_ATT_HEADS):
        cols = slice(h * HEAD_DIM, (h + 1) * HEAD_DIM)
        qo[h] = _rope(q_ref[:, cols], cos, sin)
        ko[h] = _rope(k_ref[:, cols], cos, sin)
        vo[h] = v_ref[:, cols]


def attn_prep_sample(proj, cos, sin, n_seq):
    h_n = N_ATT_HEADS
    shp = jax.ShapeDtypeStruct((n_seq, h_n, SEQ_PAD, HEAD_DIM), f32)
    col = lambda cb: pl.BlockSpec((SEQ_PAD, ATT_W), lambda b: (b, cb))
    o_spec = pl.BlockSpec((None, h_n, SEQ_PAD, HEAD_DIM), lambda b: (b, 0, 0, 0))
    tab = pl.BlockSpec((SEQ_PAD, HEAD_DIM), lambda b: (0, 0))
    return pl.pallas_call(
        _attn_prep_body,
        grid=(n_seq,),
        in_specs=[col(0), col(1), col(2), tab, tab],
        out_specs=[o_spec, o_spec, o_spec],
        out_shape=[shp, shp, shp],
        compiler_params=_cparams(("arbitrary",)),
        name="attn_prep_sample",
    )(proj, proj, proj, cos, sin)


def _top_blocks(gate, n_valid, on_pick):
    lane = lax.broadcasted_iota(jnp.int32, gate.shape, 1)
    g = jnp.where(lane < n_valid, gate, -jnp.inf)
    for r in range(MOBA_TOPK):
        m = jnp.max(g, axis=1, keepdims=True)
        idx = jnp.min(jnp.where(g == m, lane, LANES), axis=1, keepdims=True)
        on_pick(r, idx, m > -jnp.inf)
        g = jnp.where(lane == idx, -jnp.inf, g)


def _attn_prep_t_body(q_ref, k_ref, v_ref, cos_ref, sin_ref, qt_o, ko, vo, kb_o, vt_o, kmo):
    cos, sin = cos_ref[...], sin_ref[...]
    for h in range(N_ATT_HEADS):
        cols = slice(h * HEAD_DIM, (h + 1) * HEAD_DIM)
        qt_o[h] = _rope(q_ref[:, cols], cos, sin).T
        kr = _rope(k_ref[:, cols], cos, sin)
        v = v_ref[:, cols]
        ko[h] = kr
        vo[h] = v
        kb_o[h] = kr.astype(bf16)
        vt_o[h] = jnp.concatenate([v.T.astype(bf16), jnp.ones((V_ROWS - HEAD_DIM, MOBA_BLOCK), bf16)], axis=0)
        kmo[h] = jnp.mean(kr, axis=0, keepdims=True)


def attn_prep_prompt_t(proj, cos, sin):
    t = proj.shape[0]
    assert t % MOBA_BLOCK == 0
    nb = t // MOBA_BLOCK
    h_n = N_ATT_HEADS
    col = lambda cb: pl.BlockSpec((MOBA_BLOCK, ATT_W), lambda i: (i, cb))
    tab = pl.BlockSpec((MOBA_BLOCK, HEAD_DIM), lambda i: (i, 0))
    hm_spec = pl.BlockSpec((h_n, MOBA_BLOCK, HEAD_DIM), lambda i: (0, i, 0))
    t_spec = lambda rows: pl.BlockSpec((h_n, None, rows, MOBA_BLOCK), lambda i: (0, i, 0, 0))
    hm = lambda dt: jax.ShapeDtypeStruct((h_n, t, HEAD_DIM), dt)
    tr = lambda rows, dt: jax.ShapeDtypeStruct((h_n, nb, rows, MOBA_BLOCK), dt)
    qt, k, v, kb, vt, km = pl.pallas_call(
        _attn_prep_t_body,
        grid=(nb,),
        in_specs=[col(0), col(1), col(2), tab, tab],
        out_specs=[t_spec(HEAD_DIM), hm_spec, hm_spec, hm_spec, t_spec(V_ROWS),
                   pl.BlockSpec((h_n, None, 1, HEAD_DIM), lambda i: (0, i, 0, 0))],
        out_shape=[tr(HEAD_DIM, f32), hm(f32), hm(f32), hm(bf16), tr(V_ROWS, bf16),
                   jax.ShapeDtypeStruct((h_n, nb, 1, HEAD_DIM), f32)],
        compiler_params=_cparams(("arbitrary",)),
        name="attn_prep_prompt",
    )(proj, proj, proj, cos, sin)
    return qt, k, v, kb, vt, km.reshape(h_n, nb, HEAD_DIM)


def _attn_prompt_t_body(qt_ref, kb_ref, vt_ref, km_ref, o_ref, m_scr, acc_scr, *, n_blk):
    i = pl.program_id(1)
    blk = MOBA_BLOCK
    qt = qt_ref[...]
    gate = jnp.dot(km_ref[...], qt, precision=HIGHEST, preferred_element_type=f32)
    sub = lax.broadcasted_iota(jnp.int32, gate.shape, 0)
    g = jnp.where(sub < i, gate, -jnp.inf)
    sel = jnp.zeros(gate.shape, f32)
    for _ in range(MOBA_TOPK):
        m = jnp.max(g, axis=0, keepdims=True)
        idx = jnp.min(jnp.where(g == m, sub, LANES), axis=0, keepdims=True)
        pick = sub == idx
        sel = jnp.where(pick, jnp.where(m > -jnp.inf, 1.0, sel), sel)
        g = jnp.where(pick, -jnp.inf, g)
    bias = jnp.where(sel > 0.0, 0.0, NEG)
    q_aug = jnp.concatenate([(qt * (HEAD_DIM ** -0.5 * LOG2_E)).astype(bf16), bias.astype(bf16)], axis=0)

    m_scr[...] = jnp.full(m_scr.shape, NEG, f32)
    acc_scr[...] = jnp.zeros_like(acc_scr)
    lane = lax.broadcasted_iota(jnp.int32, (blk, LANES), 1)

    def blocks(work):
        scores = []
        for st, j_blk, k_extra, mask in work:
            start = pl.multiple_of(j_blk * blk, blk)
            k_aug = jnp.concatenate([kb_ref[pl.ds(start, blk), :], k_extra], axis=1)
            s = jnp.dot(k_aug, q_aug, preferred_element_type=f32)
            scores.append(s if mask is None else jnp.where(mask, s, NEG))
        probs, alphas = [], []
        for (st, _, _, _), s in zip(work, scores):
            m_old = m_scr[st]
            m_new = jnp.maximum(m_old, jnp.max(s, axis=0, keepdims=True))
            alpha = jnp.exp2(m_old - m_new)
            m_scr[st] = m_new
            probs.append(jnp.exp2(s - m_new).astype(bf16))
            alphas.append(alpha)
        pvs = [jnp.dot(vt_ref[j_blk], p, preferred_element_type=f32)
               for (_, j_blk, _, _), p in zip(work, probs)]
        for (st, _, _, _), alpha, pv in zip(work, alphas, pvs):
            acc_scr[st] = alpha * acc_scr[st] + pv

    def past_blocks(it, carry):
        work = []
        for st in range(ATTN_STREAMS):
            j = it * ATTN_STREAMS + st
            work.append((st, jnp.minimum(j, n_blk - 1), jnp.where(lane == j, 1.0, 0.0).astype(bf16), None))
        blocks(work)
        return carry

    lax.fori_loop(0, (i + ATTN_STREAMS - 1) // ATTN_STREAMS, past_blocks, 0)
    key = lax.broadcasted_iota(jnp.int32, (blk, blk), 0)
    qry = lax.broadcasted_iota(jnp.int32, (blk, blk), 1)
    blocks([(0, i, jnp.zeros((blk, LANES), bf16), key <= qry)])
    m_fin = m_scr[0]
    for st in range(1, ATTN_STREAMS):
        m_fin = jnp.maximum(m_fin, m_scr[st])
    acc = jnp.zeros((V_ROWS, blk), f32)
    for st in range(ATTN_STREAMS):
        acc = acc + jnp.exp2(m_scr[st] - m_fin) * acc_scr[st]
    o_ref[...] = (acc[:HEAD_DIM] / acc[HEAD_DIM:HEAD_DIM + 1]).T.astype(o_ref.dtype)


def attn_prompt_t(qt, kb, vt, km):
    h_n, nb = qt.shape[:2]
    t = nb * MOBA_BLOCK
    assert nb + ATTN_STREAMS <= LANES and HEAD_DIM == LANES
    km_pad = jnp.pad(km, ((0, 0), (0, LANES - nb), (0, 0)))
    stat = pltpu.VMEM((ATTN_STREAMS, 1, MOBA_BLOCK), f32)
    return pl.pallas_call(
        functools.partial(_attn_prompt_t_body, n_blk=nb),
        grid=(h_n, nb),
        in_specs=[pl.BlockSpec((None, None, HEAD_DIM, MOBA_BLOCK), lambda h, i: (h, i, 0, 0)),
                  pl.BlockSpec((None, t, HEAD_DIM), lambda h, i: (h, 0, 0)),
                  pl.BlockSpec((None, nb, V_ROWS, MOBA_BLOCK), lambda h, i: (h, 0, 0, 0)),
                  pl.BlockSpec((None, LANES, HEAD_DIM), lambda h, i: (h, 0, 0))],
        out_specs=pl.BlockSpec((MOBA_BLOCK, HEAD_DIM), lambda h, i: (i, h)),
        out_shape=jax.ShapeDtypeStruct((t, h_n * HEAD_DIM), bf16),
        scratch_shapes=[stat, pltpu.VMEM((ATTN_STREAMS, V_ROWS, MOBA_BLOCK), f32)],
        compiler_params=_cparams(("arbitrary", "arbitrary")),
        name="attn_prompt",
    )(qt, kb, vt, km_pad)


def _sample_select_body(pt_ref, *refs, n_blk, ppb, bps):
    n_pg = ppb * bps
    k_refs, q_ref, sel_ref, km_scr = refs[:n_pg], refs[n_pg], refs[n_pg + 1], refs[n_pg + 2]
    j = pl.program_id(1)

    @pl.when(j == 0)
    def _():
        km_scr[...] = jnp.zeros_like(km_scr)

    for bi in range(bps):
        ks = k_refs[bi * ppb][...].sum(axis=1)
        for p in range(1, ppb):
            ks = ks + k_refs[bi * ppb + p][...].sum(axis=1)
        ks = ks * (1.0 / MOBA_BLOCK)
        for h in range(N_ATT_HEADS):
            km_scr[h, pl.ds(j * bps + bi, 1), :] = ks[h:h + 1, :]

    @pl.when(j == n_blk // bps - 1)
    def _():
        lane = lax.broadcasted_iota(jnp.int32, (SEQ_PAD, LANES), 1)
        sel = [jnp.zeros((SEQ_PAD, LANES), jnp.int32)]
        for h in range(N_ATT_HEADS):
            gate = _dot_nt(q_ref[h], km_scr[h], HIGHEST)

            def on_pick(r, idx, has, h=h):
                sel[0] = jnp.where(lane == h * MOBA_TOPK + r, idx, sel[0])

            _top_blocks(gate, n_blk, on_pick)
        sel_ref[...] = sel[0]


def sample_select(cache_k_hm, layer, page_table, q_s):
    n_seq, n_pages = page_table.shape
    page = cache_k_hm.shape[3]
    ppb = MOBA_BLOCK // page
    assert ppb * page == MOBA_BLOCK and n_pages % ppb == 0
    n_blk = n_pages // ppb
    assert MOBA_TOPK <= n_blk <= LANES
    bps = next(c for c in (8, 4, 2, 1) if n_blk % c == 0)
    h_n = N_ATT_HEADS

    def page_spec(p):
        return pl.BlockSpec((None, None, h_n, page, HEAD_DIM),
                            lambda b, j, pt: (layer, pt[b, j * (bps * ppb) + p], 0, 0, 0))

    return pl.pallas_call(
        functools.partial(_sample_select_body, n_blk=n_blk, ppb=ppb, bps=bps),
        grid_spec=pltpu.PrefetchScalarGridSpec(
            num_scalar_prefetch=1, grid=(n_seq, n_blk // bps),
            in_specs=[page_spec(p) for p in range(bps * ppb)]
                     + [pl.BlockSpec((None, h_n, SEQ_PAD, HEAD_DIM), lambda b, j, pt: (b, 0, 0, 0))],
            out_specs=pl.BlockSpec((None, SEQ_PAD, LANES), lambda b, j, pt: (b, 0, 0)),
            scratch_shapes=[pltpu.VMEM((h_n, LANES, HEAD_DIM), f32)]),
        out_shape=jax.ShapeDtypeStruct((n_seq, SEQ_PAD, LANES), jnp.int32),
        compiler_params=_cparams(("arbitrary", "arbitrary")),
        name="sample_select",
    )(page_table, *([cache_k_hm] * (bps * ppb)), q_s)


def _sample_attn_body(pt_ref, sel_ref, ck_ref, cv_ref, q_ref, kn_ref, vn_ref, o_ref,
                      kbuf, vbuf, sem, *, layer, t_new, ppb, page, n_seq):
    b = pl.program_id(0)
    slot = b % 2
    h_n = N_ATT_HEADS
    heads = range(h_n)

    def copies(seq, sl, t):
        out = []
        for h in heads:
            th = t * h_n + h
            for r in range(MOBA_TOPK):
                blk = sel_ref[seq, th * MOBA_TOPK + r]
                for p in range(ppb):
                    pg = pt_ref[seq, blk * ppb + p]
                    dst = pl.ds((r * ppb + p) * page, page)
                    out.append(pltpu.make_async_copy(ck_ref.at[layer, pg, h], kbuf.at[sl, th, dst],
                                                     sem.at[sl, t, 0]))
                    out.append(pltpu.make_async_copy(cv_ref.at[layer, pg, h], vbuf.at[sl, th, dst],
                                                     sem.at[sl, t, 1]))
        return out

    def start_seq(seq, sl):
        for t in range(t_new):
            for cp in copies(seq, sl, t):
                cp.start()

    @pl.when(b == 0)
    def _():
        start_seq(b, slot)

    @pl.when(b + 1 < n_seq)
    def _():
        start_seq(b + 1, 1 - slot)

    o_ref[...] = jnp.zeros_like(o_ref)
    rown = lax.broadcasted_iota(jnp.int32, (SEQ_PAD, 1), 0)
    for t in range(t_new):
        for cp in copies(b, slot, t):
            cp.wait()
        qrow = [q_ref[h, t:t + 1, :] * (HEAD_DIM ** -0.5) for h in heads]
        s = [jnp.sum(kbuf[slot, t * h_n + h] * qrow[h], axis=1, keepdims=True) for h in heads]
        s_new = [jnp.where(rown <= t, jnp.sum(kn_ref[h] * qrow[h], axis=1, keepdims=True), NEG)
                 for h in heads]
        m = [jnp.maximum(jnp.max(s[h], axis=0, keepdims=True), jnp.max(s_new[h], axis=0, keepdims=True))
             for h in heads]
        e = [jnp.exp(s[h] - m[h]) for h in heads]
        e_new = [jnp.exp(s_new[h] - m[h]) for h in heads]
        den = [jnp.sum(e[h], axis=0, keepdims=True) + jnp.sum(e_new[h], axis=0, keepdims=True)
               for h in heads]
        num = [jnp.sum(e[h] * vbuf[slot, t * h_n + h], axis=0, keepdims=True)
               + jnp.sum(e_new[h] * vn_ref[h], axis=0, keepdims=True) for h in heads]
        for h in heads:
            o_ref[t:t + 1, h * HEAD_DIM:(h + 1) * HEAD_DIM] = num[h] / den[h]


def sample_attention(cache_k_hm, cache_v_hm, layer, page_table, sel, q_s, k_s, v_s, t_new):
    n_seq = page_table.shape[0]
    page = cache_k_hm.shape[3]
    ppb = MOBA_BLOCK // page
    h_n = N_ATT_HEADS
    n_sel = MOBA_TOPK * MOBA_BLOCK
    any_spec = pl.BlockSpec(memory_space=pl.ANY)
    seq_spec = pl.BlockSpec((None, h_n, SEQ_PAD, HEAD_DIM), lambda b, pt, sl: (b, 0, 0, 0))
    return pl.pallas_call(
        functools.partial(_sample_attn_body, layer=layer, t_new=t_new, ppb=ppb, page=page, n_seq=n_seq),
        grid_spec=pltpu.PrefetchScalarGridSpec(
            num_scalar_prefetch=2, grid=(n_seq,),
            in_specs=[any_spec, any_spec, seq_spec, seq_spec, seq_spec],
            out_specs=pl.BlockSpec((None, SEQ_PAD, h_n * HEAD_DIM), lambda b, pt, sl: (b, 0, 0)),
            scratch_shapes=[pltpu.VMEM((2, t_new * h_n, n_sel, HEAD_DIM), f32),
                            pltpu.VMEM((2, t_new * h_n, n_sel, HEAD_DIM), f32),
                            pltpu.SemaphoreType.DMA((2, t_new, 2))]),
        out_shape=jax.ShapeDtypeStruct((n_seq, SEQ_PAD, h_n * HEAD_DIM), f32),
        compiler_params=_cparams(("arbitrary",)),
        name="sample_attention",
    )(page_table, sel, cache_k_hm, cache_v_hm, q_s, k_s, v_s)


def _halo_cat(prev_ref, init_ref, x_ref):
    halo = jnp.where(pl.program_id(1) == 0, init_ref[...], prev_ref[...])
    return jnp.concatenate([halo, x_ref[...]], axis=0)


def _halo_specs(n_tiles, tm, halo, width, x_col, init_col):
    assert tm % halo == 0 or n_tiles == 1
    per = tm // halo

    def prev_map(b, i, *r):
        return (jnp.maximum((b * n_tiles + i) * per - 1, 0), x_col(*r))

    return [pl.BlockSpec((halo, width), prev_map),
            pl.BlockSpec((None, halo, width), lambda b, i, *r: (b, 0, init_col(*r))),
            pl.BlockSpec((tm, width), lambda b, i, *r: (b * n_tiles + i, x_col(*r)))]


def _gdn_prep_body(prev_ref, init_ref, x_ref, ab_ref, cw_ref, alog_ref, dtb_ref,
                   q_o, k_o, v_o, gb_o, *, tm, t_valid):
    cat = _halo_cat(prev_ref, init_ref, x_ref)
    w = cw_ref[...]
    acc = x_ref[...] * w[GDN_CONV - 1:GDN_CONV, :]
    for s in range(1, GDN_CONV):
        acc = acc + pltpu.roll(cat, s, axis=0)[SUBLANES:] * w[GDN_CONV - 1 - s:GDN_CONV - s, :]
    y = _silu(acc)
    row = pl.program_id(1) * tm + lax.broadcasted_iota(jnp.int32, (tm, 1), 0)
    valid = row < t_valid
    for h in range(N_GDN_HEADS):
        lo = h * HEAD_DIM
        qh = y[:, lo:lo + HEAD_DIM]
        kh = y[:, GDN_W + lo:GDN_W + lo + HEAD_DIM]
        qn = qh * lax.rsqrt(jnp.sum(qh * qh, axis=-1, keepdims=True) + EPS) * (HEAD_DIM ** -0.5)
        kn = kh * lax.rsqrt(jnp.sum(kh * kh, axis=-1, keepdims=True) + EPS)
        q_o[:, lo:lo + HEAD_DIM] = jnp.where(valid, qn, 0.0)
        k_o[:, lo:lo + HEAD_DIM] = jnp.where(valid, kn, 0.0)
    v_o[...] = jnp.where(valid, y[:, 2 * GDN_W:], 0.0)
    ab = ab_ref[...]
    g = -jnp.exp(alog_ref[...]) * _softplus(ab + dtb_ref[...])
    lane = lax.broadcasted_iota(jnp.int32, ab.shape, 1)
    gb = jnp.where(lane < N_GDN_HEADS, g, jnp.where(lane < 2 * N_GDN_HEADS, _sigmoid(ab), 0.0))
    gb_o[...] = jnp.where(valid, gb, 0.0)


def gdn_prep(proj, init, conv_w, a_log, dt_bias, *, n_seq, t_pad, t_valid, tm_pref=256):
    m = proj.shape[0]
    tm = _tile(t_pad, tm_pref)
    n_tiles = t_pad // tm
    w3 = 3 * GDN_W
    pad = lambda a: jnp.pad(a.reshape(1, -1), ((0, 0), (0, LANES - a.shape[-1])))
    ab_blk = COL_AB // LANES
    out = jax.ShapeDtypeStruct((m, GDN_W), f32)
    row_spec = lambda wd: pl.BlockSpec((tm, wd), lambda b, i: (b * n_tiles + i, 0))
    const = lambda r, c: pl.BlockSpec((r, c), lambda b, i: (0, 0))
    return pl.pallas_call(
        functools.partial(_gdn_prep_body, tm=tm, t_valid=t_valid),
        grid=(n_seq, n_tiles),
        in_specs=_halo_specs(n_tiles, tm, SUBLANES, w3, lambda: COL_QKV_B // w3, lambda: 0)
                 + [pl.BlockSpec((tm, LANES), lambda b, i: (b * n_tiles + i, ab_blk)),
                    const(GDN_CONV, w3), const(1, LANES), const(1, LANES)],
        out_specs=[row_spec(GDN_W), row_spec(GDN_W), row_spec(GDN_W), row_spec(LANES)],
        out_shape=[out, out, out, jax.ShapeDtypeStruct((m, LANES), f32)],
        compiler_params=_cparams(("arbitrary", "arbitrary")),
        name="gdn_prep",
    )(proj, init, proj, proj, conv_w, pad(a_log), pad(dt_bias))


def _gdn_intra_body(q_ref, k_ref, v_ref, gb_ref, qe_ref, o0_ref, m_ref, b_ref, gl_ref, *, rt):
    cs = GDN_CHUNK
    nc = rt // cs
    gb = gb_ref[...]
    r = lax.broadcasted_iota(jnp.int32, (rt, rt), 0)
    c = lax.broadcasted_iota(jnp.int32, (rt, rt), 1)
    in_chunk = c >= r - r % cs
    causal_f = jnp.where(c <= r, jnp.where(in_chunk, 1.0, 0.0), 0.0)
    strict_f = jnp.where(c < r, causal_f, 0.0)
    causal, strict = causal_f > 0.0, strict_f > 0.0
    eye = jnp.where(r == c, 1.0, 0.0)
    gcum = jnp.dot(causal_f, gb, precision=HIGHEST, preferred_element_type=f32)
    gl_rows = jnp.concatenate(
        [jnp.broadcast_to(gcum[(j + 1) * cs - 1:(j + 1) * cs, :], (cs, LANES)) for j in range(nc)], axis=0)
    for j in range(nc):
        gl_ref[j] = jnp.exp(gl_rows[j * cs:j * cs + SUBLANES, :])
    heads = range(N_GDN_HEADS)
    cols = [slice(h * HEAD_DIM, (h + 1) * HEAD_DIM) for h in heads]
    q, k, v = ([ref[:, cl] for cl in cols] for ref in (q_ref, k_ref, v_ref))
    gcol = [gcum[:, h:h + 1] for h in heads]
    bcol = [gb[:, N_GDN_HEADS + h:N_GDN_HEADS + h + 1] for h in heads]

    def decay_of(gc):
        gi = jnp.broadcast_to(gc, (rt, rt))
        grow = jnp.sum(jnp.where(r == c, gi, 0.0), axis=0, keepdims=True)
        return jnp.exp(jnp.where(causal, gi - grow, NEG))

    decay = [decay_of(gcol[h]) for h in heads]
    kb = [k[h].astype(bf16) for h in heads]
    a = [jnp.where(strict, bcol[h] * _dot_nt(kb[h], kb[h]) * decay[h], 0.0) for h in heads]
    def in_blocks(size):
        return (r // size) == (c // size)

    a_in = [jnp.where(in_blocks(GDN_BASE), a[h], 0.0) for h in heads]
    x = [eye - a_in[h] for h in heads]
    p = [_dot3(a_in[h], a_in[h]) for h in heads]
    n_sq = max(1, (GDN_BASE - 1).bit_length() - 1)
    for it in range(n_sq):
        x = [x[h] + _dot3(x[h], p[h]) for h in heads]
        if it + 1 < n_sq:
            p = [_dot3(p[h], p[h]) for h in heads]
    size = GDN_BASE
    while size < cs:
        size *= 2
        a_out = [jnp.where(in_blocks(size), a[h], 0.0) - a_in[h] for h in heads]
        x = [x[h] - _dot3(_dot3(x[h], a_out[h]), x[h]) for h in heads]
        a_in = [a_in[h] + a_out[h] for h in heads]
    eg = [jnp.exp(gcol[h]) for h in heads]
    sol = [_dot3(x[h], jnp.concatenate([v[h] * bcol[h], k[h] * (bcol[h] * eg[h])], axis=1))
           for h in heads]
    qk = [jnp.where(causal, _dot_nt(q[h].astype(bf16), kb[h]) * decay[h], 0.0) for h in heads]
    qs = [_bdot(qk[h], sol[h]) for h in heads]
    for h in heads:
        o0_ref[:, cols[h]] = qs[h][:, :HEAD_DIM]
        qe_ref[:, cols[h]] = q[h] * eg[h] - qs[h][:, HEAD_DIM:]
    for h in heads:
        kt = (k[h] * jnp.exp(gl_rows[:, h:h + 1] - gcol[h])).astype(bf16)
        solb = sol[h].astype(bf16)
        for j in range(nc):
            rows = slice(j * cs, (j + 1) * cs)
            mb = _dot_tn(kt[rows], solb[rows])
            b_ref[j, h] = mb[:, :HEAD_DIM]
            m_ref[j, h] = -mb[:, HEAD_DIM:]


def _gdn_scan_body(qe_ref, o0_ref, m_ref, b_ref, gl_ref, z_ref, nw_ref, s0_ref, o_ref, s_out_ref, s_scr,
                   *, cps, n_steps):
    step = pl.program_id(1)
    cs = GDN_CHUNK

    @pl.when(step == 0)
    def _():
        s_scr[...] = s0_ref[...]

    for j in range(cps):
        rows = slice(j * cs, (j + 1) * cs)
        for h in range(N_GDN_HEADS):
            cols = slice(h * HEAD_DIM, (h + 1) * HEAD_DIM)
            s = s_scr[h]
            sb = s.astype(bf16)
            o = jnp.dot(qe_ref[rows, cols].astype(bf16), sb, preferred_element_type=f32) + o0_ref[rows, cols]
            s_scr[h] = (s * gl_ref[j, 0:1, h:h + 1]
                        + jnp.dot(m_ref[j, h].astype(bf16), sb, preferred_element_type=f32) + b_ref[j, h])
            on = o * lax.rsqrt(jnp.mean(o * o, axis=-1, keepdims=True) + EPS) * nw_ref[...]
            o_ref[rows, cols] = (on * _silu(z_ref[rows, cols])).astype(o_ref.dtype)

    @pl.when(step == n_steps - 1)
    def _():
        s_out_ref[...] = s_scr[...]


def gdn_recurrence(q, k, v, gb, z, z_blk, norm_w, s0, s0_layer, *, n_seq, t_pad):
    m = q.shape[0]
    cs = GDN_CHUNK
    assert t_pad % cs == 0
    rt = _tile(m, GDN_TILE)
    assert rt % cs == 0
    nct = rt // cs
    nc_all = m // cs
    h_n = N_GDN_HEADS
    rows = lambda wd: pl.BlockSpec((rt, wd), lambda i: (i, 0))
    mats = pl.BlockSpec((nct, h_n, HEAD_DIM, HEAD_DIM), lambda i: (i, 0, 0, 0))
    mat_shape = jax.ShapeDtypeStruct((nc_all, h_n, HEAD_DIM, HEAD_DIM), f32)
    qe, o0, mm, bb, gl = pl.pallas_call(
        functools.partial(_gdn_intra_body, rt=rt),
        grid=(m // rt,),
        in_specs=[rows(GDN_W), rows(GDN_W), rows(GDN_W), rows(LANES)],
        out_specs=[rows(GDN_W), rows(GDN_W), mats, mats,
                   pl.BlockSpec((nct, SUBLANES, LANES), lambda i: (i, 0, 0))],
        out_shape=[jax.ShapeDtypeStruct((m, GDN_W), f32), jax.ShapeDtypeStruct((m, GDN_W), f32),
                   mat_shape, mat_shape, jax.ShapeDtypeStruct((nc_all, SUBLANES, LANES), f32)],
        compiler_params=_cparams(("arbitrary",)),
        name="gdn_intra",
    )(q, k, v, gb)

    n_chunks = t_pad // cs
    cps = next(c for c in (4, 2, 1) if n_chunks % c == 0)
    n_steps = n_chunks // cps
    srow = lambda wd, cb=0: pl.BlockSpec((cps * cs, wd), lambda b, s: (b * n_steps + s, cb))
    smat = pl.BlockSpec((cps, h_n, HEAD_DIM, HEAD_DIM), lambda b, s: (b * n_steps + s, 0, 0, 0))
    return pl.pallas_call(
        functools.partial(_gdn_scan_body, cps=cps, n_steps=n_steps),
        grid=(n_seq, n_steps),
        in_specs=[srow(GDN_W), srow(GDN_W), smat, smat,
                  pl.BlockSpec((cps, SUBLANES, LANES), lambda b, s: (b * n_steps + s, 0, 0)),
                  srow(GDN_W, z_blk), pl.BlockSpec((1, HEAD_DIM), lambda b, s: (0, 0)),
                  pl.BlockSpec((None, None, h_n, HEAD_DIM, HEAD_DIM), lambda b, s: (s0_layer, b, 0, 0, 0))],
        out_specs=[srow(GDN_W),
                   pl.BlockSpec((None, h_n, HEAD_DIM, HEAD_DIM), lambda b, s: (b, 0, 0, 0))],
        out_shape=[jax.ShapeDtypeStruct((m, GDN_W), bf16),
                   jax.ShapeDtypeStruct((n_seq, h_n, HEAD_DIM, HEAD_DIM), f32)],
        scratch_shapes=[pltpu.VMEM((h_n, HEAD_DIM, HEAD_DIM), f32)],
        compiler_params=_cparams(("arbitrary", "arbitrary")),
        name="gdn_scan",
    )(qe, o0, mm, bb, gl, z, norm_w.reshape(1, HEAD_DIM), s0)


def _pool_body(prev_ref, init_ref, x_ref, pw_ref, ps_ref, o_ref, *, tm, start):
    cat = _halo_cat(prev_ref, init_ref, x_ref)
    sums, s, span = [], cat, 1
    for _ in POOL_WINDOWS:
        s = s + pltpu.roll(s, span, axis=0)
        span *= 2
        sums.append(s)
    assert span == POOL_HALO
    pos = start + pl.program_id(1) * tm + lax.broadcasted_iota(jnp.int32, (tm, 1), 0)
    x = x_ref[...]
    for g, win in enumerate(POOL_WINDOWS):
        lo = g * POOL_GROUP_W
        cnt = jnp.minimum(win, pos + 1).astype(f32)
        mix = sums[g][POOL_HALO:, lo:lo + POOL_GROUP_W] / cnt - x[:, lo:lo + POOL_GROUP_W]
        out = jnp.dot(mix, pw_ref[g], precision=HIGHEST, preferred_element_type=f32)
        o_ref[:, lo:lo + POOL_GROUP_W] = (out * ps_ref[:, lo:lo + POOL_GROUP_W]).astype(o_ref.dtype)


def pool_mix(proj, init, pool_w, pool_scale, *, n_seq, t_pad, start, tm_pref=512):
    m = proj.shape[0]
    tm = _tile(t_pad, tm_pref)
    n_tiles = t_pad // tm
    return pl.pallas_call(
        functools.partial(_pool_body, tm=tm, start=start),
        grid=(n_seq, n_tiles),
        in_specs=_halo_specs(n_tiles, tm, POOL_HALO, POOL_W, lambda: COL_POOL // POOL_W, lambda: 0)
                 + [pl.BlockSpec(pool_w.shape, lambda b, i: (0, 0, 0)),
                    pl.BlockSpec((1, POOL_W), lambda b, i: (0, 0))],
        out_specs=pl.BlockSpec((tm, POOL_W), lambda b, i: (b * n_tiles + i, 0)),
        out_shape=jax.ShapeDtypeStruct((m, POOL_W), _act_dtype(tm)),
        compiler_params=_cparams(("arbitrary", "arbitrary")),
        name="pool_mix",
    )(proj, init, proj, pool_w, pool_scale.reshape(1, POOL_W))


def _conv_act_body(prev_ref, init_ref, a_ref, u_ref, cw_ref, o_ref):
    cat = _halo_cat(prev_ref, init_ref, a_ref)
    w = cw_ref[...]
    acc = a_ref[...] * w[FFN_CONV - 1:FFN_CONV, :]
    for s in range(1, FFN_CONV):
        acc = acc + pltpu.roll(cat, s, axis=0)[SUBLANES:] * w[FFN_CONV - 1 - s:FFN_CONV - s, :]
    o_ref[...] = (_silu(acc) * u_ref[...]).astype(o_ref.dtype)


def conv_act(a, u, init, conv_w, *, n_seq, t_pad, tm_pref=512, tile_elems=512 * 512):
    m, f = a.shape
    tm = _tile(t_pad, tm_pref)
    tf = _tile(f, max(LANES, tile_elems // tm // LANES * LANES))
    n_tiles = t_pad // tm
    return pl.pallas_call(
        _conv_act_body,
        grid=(n_seq, n_tiles, f // tf),
        in_specs=_halo_specs(n_tiles, tm, SUBLANES, tf, lambda j: j, lambda j: j)
                 + [pl.BlockSpec((tm, tf), lambda b, i, j: (b * n_tiles + i, j)),
                    pl.BlockSpec((FFN_CONV, tf), lambda b, i, j: (0, j))],
        out_specs=pl.BlockSpec((tm, tf), lambda b, i, j: (b * n_tiles + i, j)),
        out_shape=jax.ShapeDtypeStruct((m, f), _act_dtype(tm)),
        compiler_params=_cparams(("arbitrary", "arbitrary", "arbitrary")),
        name="conv_act",
    )(a, init, a, u, conv_w)


def _rope_tables(pos):
    half = HEAD_DIM // 2
    inv = ROPE_THETA ** (-jnp.arange(half, dtype=f32) * 2.0 / HEAD_DIM)
    ang = pos.astype(f32)[:, None] * inv[None, :]
    cos, sin = jnp.cos(ang), jnp.sin(ang)
    return jnp.concatenate([cos, cos], axis=-1), jnp.concatenate([-sin, sin], axis=-1)


def _reorder_w_in(w_in):
    qkv = 3 * ATT_W + 3 * GDN_W
    z_lo = qkv
    ab_lo = z_lo + GDN_W
    pool_lo = ab_lo + 2 * N_GDN_HEADS
    assert w_in.shape[-1] == pool_lo + POOL_W
    zeros = jnp.zeros(w_in.shape[:-1] + (COL_Z - COL_AB - 2 * N_GDN_HEADS,), w_in.dtype)
    w = jnp.concatenate([w_in[..., :qkv], w_in[..., pool_lo:], w_in[..., ab_lo:pool_lo], zeros,
                         w_in[..., z_lo:ab_lo]], axis=-1)
    return w.astype(bf16)


def _tail_rows(buf, x, t_valid, lo, hi, n):
    if t_valid >= n:
        return x[:, t_valid - n:t_valid, lo:hi]
    return jnp.concatenate([buf[:, t_valid:], x[:, :t_valid, lo:hi]], axis=1)


def _front_pad(state, rows):
    return jnp.pad(state, ((0, 0), (rows - state.shape[1], 0), (0, 0)))


def _group_layer(x, mods, layer, lw, *, n_seq, t_pad, t_valid, start, states, attn_fn):
    (g_mix, g_ffn, w_in_r, gdn_conv_w, gdn_a_log, gdn_dt_bias, gdn_norm_w, pool_w, pool_scale,
     w_out, w_gate, w_up, ffn_conv_w, w_down) = lw
    sh1, sc1, gt1, sh2, sc2, gt2 = mods
    s_gdn, s_gdn_layer, gdn_buf, pool_buf, ffn_buf = states
    m = x.shape[0]
    (proj,) = matmul(norm_mod(x, g_mix, sc1, sh1), [w_in_r], layer, name="in_proj")
    proj3 = proj.reshape(n_seq, t_pad, IN_W_PAD)

    o_a, k_new, v_new = attn_fn(proj)

    qn, kn, vc, gb = gdn_prep(proj, _front_pad(gdn_buf, SUBLANES), gdn_conv_w, gdn_a_log, gdn_dt_bias,
                              n_seq=n_seq, t_pad=t_pad, t_valid=t_valid)
    t_gdn = -(-t_pad // GDN_CHUNK) * GDN_CHUNK
    if t_gdn == t_pad:
        o_b, s_new = gdn_recurrence(qn, kn, vc, gb, proj, COL_Z // GDN_W, gdn_norm_w, s_gdn, s_gdn_layer,
                                    n_seq=n_seq, t_pad=t_pad)
    else:
        def chunk_pad(a):
            a = a.reshape(n_seq, t_pad, a.shape[-1])
            return jnp.pad(a, ((0, 0), (0, t_gdn - t_pad), (0, 0))).reshape(n_seq * t_gdn, -1)
        o_b, s_new = gdn_recurrence(chunk_pad(qn), chunk_pad(kn), chunk_pad(vc), chunk_pad(gb),
                                    chunk_pad(proj[:, COL_Z:COL_Z + GDN_W]), 0, gdn_norm_w, s_gdn,
                                    s_gdn_layer, n_seq=n_seq, t_pad=t_gdn)
        o_b = o_b.reshape(n_seq, t_gdn, GDN_W)[:, :t_pad].reshape(m, GDN_W)
    new_gdn_buf = _tail_rows(gdn_buf, proj3, t_valid, COL_QKV_B, COL_QKV_B + 3 * GDN_W, GDN_CONV - 1)

    o_p = pool_mix(proj, _front_pad(pool_buf, POOL_HALO), pool_w, pool_scale,
                   n_seq=n_seq, t_pad=t_pad, start=start)
    new_pool_buf = _tail_rows(pool_buf, proj3, t_valid, COL_POOL, COL_POOL + POOL_W, POOL_BUF)

    mixed = jnp.concatenate([o.astype(bf16) for o in (o_a, o_b, o_p)], axis=1)
    (x,) = matmul(mixed, [w_out], layer, x=x, gt=gt1, name="out_proj")

    h2 = norm_mod(x, g_ffn, sc2, sh2)
    init_a = _front_pad(ffn_buf, SUBLANES)
    if n_seq == 1 and t_valid == t_pad and t_pad >= SUBLANES:
        act, a_tail = ffn_up_conv(h2, w_gate, w_up, layer, init_a[0], ffn_conv_w)
        new_ffn_buf = a_tail[None, SUBLANES - (FFN_CONV - 1):]
    else:
        a, u = matmul(h2, [w_gate, w_up], layer, name="ffn_up")
        act = conv_act(a, u, init_a, ffn_conv_w, n_seq=n_seq, t_pad=t_pad)
        f = a.shape[1]
        new_ffn_buf = _tail_rows(ffn_buf, a.reshape(n_seq, t_pad, f), t_valid, 0, f, FFN_CONV - 1)
    (x,) = matmul(act, [w_down], layer, x=x, gt=gt2, tm_pref=512, name="ffn_down")
    return x, k_new, v_new, s_new, new_gdn_buf, new_pool_buf, new_ffn_buf


def kernel(x_prompt, x_sample, cache_k, cache_v, page_table, state_gdn, state_gdn_conv, state_pool,
           state_ffn_conv, c_prompt, c_sample, w_ada, b_ada, g_mix, g_ffn, w_in, gdn_conv_w, gdn_a_log,
           gdn_dt_bias, gdn_norm_w, pool_w, pool_scale, w_out, w_gate, w_up, ffn_conv_w, w_down, g_final):
    bp, t_p, d = x_prompt.shape
    bs, t_s, _ = x_sample.shape
    depth = w_ada.shape[0]
    page = cache_k.shape[2]
    past_len = page_table.shape[1] * page
    assert bp == 1 and t_s <= SEQ_PAD and t_p % MOBA_BLOCK == 0 and past_len % MOBA_BLOCK == 0
    assert t_p >= POOL_BUF

    ck = jnp.transpose(cache_k, (0, 1, 3, 2, 4))
    cv = jnp.transpose(cache_v, (0, 1, 3, 2, 4))

    n_c = bp + bs
    mc = -(-n_c // SUBLANES) * SUBLANES
    c_all = jnp.pad(jnp.concatenate([c_prompt, c_sample], axis=0), ((0, mc - n_c), (0, 0)))
    mod = ada_mod(c_all, w_ada, b_ada)
    w_in_r = _reorder_w_in(w_in)

    cos_p, sin_p = _rope_tables(jnp.arange(t_p))
    cos_s, sin_s = _rope_tables(past_len + jnp.arange(SEQ_PAD))

    xp = x_prompt.reshape(bp * t_p, d)
    xs = jnp.pad(x_sample, ((0, 0), (0, SEQ_PAD - t_s), (0, 0))).reshape(bs * SEQ_PAD, d)

    zeros_p = lambda *shape: jnp.zeros((bp,) + shape, f32)
    outs_p, outs_s = [], []
    for l in range(depth):
        lw = (g_mix[l], g_ffn[l], w_in_r, gdn_conv_w[l], gdn_a_log[l], gdn_dt_bias[l], gdn_norm_w[l],
              pool_w[l], pool_scale[l], w_out, w_gate, w_up, ffn_conv_w[l], w_down)
        mods_p = [mod[l, 0:bp, i * d:(i + 1) * d] for i in range(6)]
        mods_s = [jnp.repeat(mod[l, bp:n_c, i * d:(i + 1) * d], SEQ_PAD, axis=0) for i in range(6)]

        def attn_p(proj):
            qt, k, v, kb, vt, km = attn_prep_prompt_t(proj, cos_p, sin_p)
            return attn_prompt_t(qt, kb, vt, km), k, v

        def attn_s(proj, l=l):
            q, k, v = attn_prep_sample(proj, cos_s, sin_s, bs)
            sel = sample_select(ck, l, page_table, q)
            sel = sel[:, :t_s, :N_ATT_HEADS * MOBA_TOPK].reshape(bs, -1)
            o = sample_attention(ck, cv, l, page_table, sel, q, k, v, t_s)
            return o.reshape(bs * SEQ_PAD, ATT_W), k[:, :, :t_s], v[:, :, :t_s]

        xp, *st_p = _group_layer(
            xp, mods_p, l, lw, n_seq=bp, t_pad=t_p, t_valid=t_p, start=0, attn_fn=attn_p,
            states=(zeros_p(N_GDN_HEADS, HEAD_DIM, HEAD_DIM)[None], 0, zeros_p(GDN_CONV - 1, 3 * GDN_W),
                    zeros_p(POOL_BUF, POOL_W), zeros_p(FFN_CONV - 1, w_gate.shape[-1])))
        xs, *st_s = _group_layer(
            xs, mods_s, l, lw, n_seq=bs, t_pad=SEQ_PAD, t_valid=t_s, start=past_len, attn_fn=attn_s,
            states=(state_gdn, l, state_gdn_conv[l], state_pool[l], state_ffn_conv[l]))
        outs_p.append(st_p)
        outs_s.append(st_s)

    y_prompt = rms_norm(xp, g_final).reshape(bp, t_p, d)
    y_sample = rms_norm(xs, g_final).reshape(bs, SEQ_PAD, d)[:, :t_s]
    kp, vp, gp, gcp, pp, fp = [jnp.stack(t) for t in zip(*outs_p)]
    ks, vs, gs, gcs, ps, fs = [jnp.stack(t) for t in zip(*outs_s)]
    k_prompt = jnp.transpose(kp, (0, 2, 1, 3))[:, None]
    v_prompt = jnp.transpose(vp, (0, 2, 1, 3))[:, None]
    k_sample = jnp.transpose(ks, (0, 1, 3, 2, 4))
    v_sample = jnp.transpose(vs, (0, 1, 3, 2, 4))
    return (y_prompt, y_sample, k_prompt, v_prompt, k_sample, v_sample, gp, gs, gcp, gcs, pp, ps, fp, fs)
```

```python
import functools

import jax
import jax.numpy as jnp
from jax import lax
from jax.experimental import pallas as pl
from jax.experimental.pallas import tpu as pltpu

f32 = jnp.float32
bf16 = jnp.bfloat16
HIGHEST = lax.Precision.HIGHEST

LANES = 128
SUBLANES = 8
VMEM_LIMIT_BYTES = 56 * 1024 * 1024

HEAD_DIM = 128
N_ATT_HEADS = 6
N_GDN_HEADS = 6
ATT_W = N_ATT_HEADS * HEAD_DIM
GDN_W = N_GDN_HEADS * HEAD_DIM
POOL_WINDOWS = (2, 4, 8, 16)
POOL_GROUP_W = 128
POOL_W = len(POOL_WINDOWS) * POOL_GROUP_W
POOL_BUF = 15
POOL_HALO = 16
MOBA_BLOCK = 256
MOBA_TOPK = 3
ATTN_STREAMS = 8
V_ROWS = 128 + 16
GDN_CHUNK = 64
GDN_BASE = 16
GDN_TILE = 128
GDN_CONV = 4
FFN_CONV = 3
ROPE_THETA = 10000.0
EPS = 1e-6
SEQ_PAD = 8
NEG = -1e30
LOG2_E = 1.4426950408889634

COL_QKV_A = 0
COL_QKV_B = 3 * ATT_W
COL_POOL = COL_QKV_B + 3 * GDN_W
COL_AB = COL_POOL + POOL_W
COL_Z = COL_AB + 256
IN_W_PAD = COL_Z + GDN_W


def _cparams(sem):
    return pltpu.CompilerParams(dimension_semantics=sem, vmem_limit_bytes=VMEM_LIMIT_BYTES)


def _tile(n, pref):
    t = min(n, pref)
    while n % t:
        t -= SUBLANES
    assert t > 0 and n % t == 0
    return t


def _act_dtype(tm):
    return bf16 if tm % (2 * SUBLANES) == 0 else f32


def _sigmoid(x):
    return 1.0 / (1.0 + jnp.exp(-x))


def _silu(x):
    return x * _sigmoid(x)


def _softplus(x):
    return jnp.maximum(x, 0.0) + jnp.log(1.0 + jnp.exp(-jnp.abs(x)))


def _dot_nt(a, b, precision=None):
    return lax.dot_general(a, b, (((1,), (1,)), ((), ())), precision=precision,
                           preferred_element_type=f32)


def _dot_tn(a, b, precision=None):
    return lax.dot_general(a, b, (((0,), (0,)), ((), ())), precision=precision,
                           preferred_element_type=f32)


def _bdot(a, b):
    return jnp.dot(a.astype(bf16), b.astype(bf16), preferred_element_type=f32)


def _split_bf16(a):
    hi = a.astype(bf16)
    return hi, (a - hi.astype(f32)).astype(bf16)


def _dot3(a, b):
    (ah, al), (bh, bl) = _split_bf16(a), _split_bf16(b)
    d = functools.partial(jnp.dot, preferred_element_type=f32)
    return d(ah, bh) + (d(ah, bl) + d(al, bh))


def _mod_spec(mod, tm, ncols):
    if mod.shape[0] == 1:
        return pl.BlockSpec((1, ncols), lambda i, *_: (0, 0))
    return pl.BlockSpec((tm, ncols), lambda i, *_: (i, 0))


def _ada_body(c_ref, w_ref, b_ref, o_ref):
    s = _silu(c_ref[...])
    o_ref[...] = jnp.dot(s.astype(bf16), w_ref[...].astype(bf16),
                         preferred_element_type=f32) + b_ref[...]


def ada_mod(c_all, w_ada, b_ada):
    mc, d = c_all.shape
    depth, _, n = w_ada.shape
    tn = _tile(n, 1024)
    return pl.pallas_call(
        _ada_body,
        grid=(depth, n // tn),
        in_specs=[pl.BlockSpec((mc, d), lambda l, j: (0, 0)),
                  pl.BlockSpec((None, d, tn), lambda l, j: (l, 0, j)),
                  pl.BlockSpec((None, 1, tn), lambda l, j: (l, 0, j))],
        out_specs=pl.BlockSpec((None, mc, tn), lambda l, j: (l, 0, j)),
        out_shape=jax.ShapeDtypeStruct((depth, mc, n), f32),
        compiler_params=_cparams(("arbitrary", "arbitrary")),
        name="ada_mod",
    )(c_all, w_ada, b_ada.reshape(depth, 1, n))


def _norm_mod_body(x_ref, g_ref, sc_ref, sh_ref, o_ref):
    x = x_ref[...]
    y = x * lax.rsqrt(jnp.mean(x * x, axis=-1, keepdims=True) + EPS)
    o_ref[...] = ((y * g_ref[...]) * (1.0 + sc_ref[...]) + sh_ref[...]).astype(o_ref.dtype)


def norm_mod(x, g, sc, sh, *, tm_pref=512, name="norm_mod"):
    m, d = x.shape
    tm = _tile(m, tm_pref)
    return pl.pallas_call(
        _norm_mod_body,
        grid=(m // tm,),
        in_specs=[pl.BlockSpec((tm, d), lambda i: (i, 0)), pl.BlockSpec((1, d), lambda i: (0, 0)),
                  _mod_spec(sc, tm, d), _mod_spec(sh, tm, d)],
        out_specs=pl.BlockSpec((tm, d), lambda i: (i, 0)),
        out_shape=jax.ShapeDtypeStruct((m, d), bf16),
        compiler_params=_cparams(("arbitrary",)),
        name=name,
    )(x, g.reshape(1, d), sc, sh)


def _mm_body(*refs, k_sizes, n_w, residual, cast_w):
    n_a = len(k_sizes)
    a_refs, rest = refs[:n_a], refs[n_a:]
    w_refs, rest = rest[:n_w], rest[n_w:]
    if residual:
        (x_ref, gt_ref), rest = rest[:2], rest[2:]
    o_refs, w_scrs = rest[:n_w], rest[n_w:]

    if cast_w:
        @pl.when(pl.program_id(1) == 0)
        def _():
            for w_ref, w_scr in zip(w_refs, w_scrs):
                w_scr[...] = w_ref[...].astype(bf16)

    a_parts = [a_ref[...].astype(bf16) for a_ref in a_refs]
    for n_i, o_ref in enumerate(o_refs):
        w_src = w_scrs[n_i] if cast_w else w_refs[n_i]
        acc, k0 = None, 0
        for a, k_sz in zip(a_parts, k_sizes):
            part = jnp.dot(a, w_src[k0:k0 + k_sz, :], preferred_element_type=f32)
            acc = part if acc is None else acc + part
            k0 += k_sz
        o_ref[...] = x_ref[...] + gt_ref[...] * acc if residual else acc


def matmul(a, ws, layer, *, x=None, gt=None, tm_pref=1024, tn_pref=512, name="mm"):
    a_list = list(a) if isinstance(a, (list, tuple)) else [a]
    m = a_list[0].shape[0]
    k_sizes = tuple(p.shape[1] for p in a_list)
    kdim = sum(k_sizes)
    n = ws[0].shape[2]
    assert ws[0].shape[1] == kdim
    tm, tn = _tile(m, tm_pref), _tile(n, tn_pref)
    n_w = len(ws)
    residual = x is not None
    assert not residual or n_w == 1
    cast_w = ws[0].dtype != bf16
    in_specs = [pl.BlockSpec((tm, k_sz), lambda j, i: (i, 0)) for k_sz in k_sizes]
    in_specs += [pl.BlockSpec((None, kdim, tn), lambda j, i: (layer, 0, j)) for _ in ws]
    args = [*a_list, *ws]
    if residual:
        gt_spec = (pl.BlockSpec((1, tn), lambda j, i: (0, j)) if gt.shape[0] == 1
                   else pl.BlockSpec((tm, tn), lambda j, i: (i, j)))
        in_specs += [pl.BlockSpec((tm, tn), lambda j, i: (i, j)), gt_spec]
        args += [x, gt]
    outs = pl.pallas_call(
        functools.partial(_mm_body, k_sizes=k_sizes, n_w=n_w, residual=residual, cast_w=cast_w),
        grid=(n // tn, m // tm),
        in_specs=in_specs,
        out_specs=[pl.BlockSpec((tm, tn), lambda j, i: (i, j)) for _ in ws],
        out_shape=[jax.ShapeDtypeStruct((m, n), f32) for _ in ws],
        scratch_shapes=[pltpu.VMEM((kdim, tn), bf16) for _ in ws] if cast_w else [],
        compiler_params=_cparams(("arbitrary", "arbitrary")),
        name=name,
    )(*args)
    return outs


def _ffn_up_conv_body(h_ref, wg_ref, wu_ref, init_ref, cw_ref, act_ref, tail_ref,
                      wg_scr, wu_scr, prev_scr, *, n_row_tiles):
    i = pl.program_id(1)

    @pl.when(i == 0)
    def _():
        wg_scr[...] = wg_ref[...].astype(bf16)
        wu_scr[...] = wu_ref[...].astype(bf16)
        prev_scr[...] = init_ref[...]

    h = h_ref[...]
    a = jnp.dot(h, wg_scr[...], preferred_element_type=f32)
    u = jnp.dot(h, wu_scr[...], preferred_element_type=f32)
    cat = jnp.concatenate([prev_scr[...], a], axis=0)
    w = cw_ref[...]
    acc = a * w[FFN_CONV - 1:FFN_CONV, :]
    for s in range(1, FFN_CONV):
        acc = acc + pltpu.roll(cat, s, axis=0)[SUBLANES:] * w[FFN_CONV - 1 - s:FFN_CONV - s, :]
    act_ref[...] = (_silu(acc) * u).astype(act_ref.dtype)
    last = a[a.shape[0] - SUBLANES:, :]
    prev_scr[...] = last

    @pl.when(i == n_row_tiles - 1)
    def _():
        tail_ref[...] = last


def ffn_up_conv(h, w_gate, w_up, layer, init, conv_w, *, tm_pref=1024, tn_pref=512):
    m, d = h.shape
    f = w_gate.shape[2]
    tm, tn = _tile(m, tm_pref), _tile(f, tn_pref)
    n_row_tiles = m // tm
    w_spec = pl.BlockSpec((None, d, tn), lambda j, i: (layer, 0, j))
    return pl.pallas_call(
        functools.partial(_ffn_up_conv_body, n_row_tiles=n_row_tiles),
        grid=(f // tn, n_row_tiles),
        in_specs=[pl.BlockSpec((tm, d), lambda j, i: (i, 0)), w_spec, w_spec,
                  pl.BlockSpec((SUBLANES, tn), lambda j, i: (0, j)),
                  pl.BlockSpec((FFN_CONV, tn), lambda j, i: (0, j))],
        out_specs=[pl.BlockSpec((tm, tn), lambda j, i: (i, j)),
                   pl.BlockSpec((SUBLANES, tn), lambda j, i: (0, j))],
        out_shape=[jax.ShapeDtypeStruct((m, f), bf16), jax.ShapeDtypeStruct((SUBLANES, f), f32)],
        scratch_shapes=[pltpu.VMEM((d, tn), bf16), pltpu.VMEM((d, tn), bf16),
                        pltpu.VMEM((SUBLANES, tn), f32)],
        compiler_params=_cparams(("arbitrary", "arbitrary")),
        name="ffn_up_conv",
    )(h, w_gate, w_up, init, conv_w)


def _rms_body(x_ref, g_ref, o_ref):
    x = x_ref[...]
    o_ref[...] = x * lax.rsqrt(jnp.mean(x * x, axis=-1, keepdims=True) + EPS) * g_ref[...]


def rms_norm(x, g):
    m, d = x.shape
    tm = _tile(m, 512)
    return pl.pallas_call(
        _rms_body,
        grid=(m // tm,),
        in_specs=[pl.BlockSpec((tm, d), lambda i: (i, 0)), pl.BlockSpec((1, d), lambda i: (0, 0))],
        out_specs=pl.BlockSpec((tm, d), lambda i: (i, 0)),
        out_shape=jax.ShapeDtypeStruct((m, d), f32),
        compiler_params=_cparams(("arbitrary",)),
        name="final_norm",
    )(x, g.reshape(1, d))


def _rope(x, cos, sin_signed):
    return x * cos + pltpu.roll(x, HEAD_DIM // 2, axis=1) * sin_signed


def _attn_prep_body(q_ref, k_ref, v_ref, cos_ref, sin_ref, qo, ko, vo):
    cos, sin = cos_ref[...], sin_ref[...]
    for h in range(N_ATT_HEADS):
        cols = slice(h * HEAD_DIM, (h + 1) * HEAD_DIM)
        qo[h] = _rope(q_ref[:, cols], cos, sin)
        ko[h] = _rope(k_ref[:, cols], cos, sin)
        vo[h] = v_ref[:, cols]


def attn_prep_sample(proj, cos, sin, n_seq):
    h_n = N_ATT_HEADS
    shp = jax.ShapeDtypeStruct((n_seq, h_n, SEQ_PAD, HEAD_DIM), f32)
    col = lambda cb: pl.BlockSpec((SEQ_PAD, ATT_W), lambda b: (b, cb))
    o_spec = pl.BlockSpec((None, h_n, SEQ_PAD, HEAD_DIM), lambda b: (b, 0, 0, 0))
    tab = pl.BlockSpec((SEQ_PAD, HEAD_DIM), lambda b: (0, 0))
    return pl.pallas_call(
        _attn_prep_body,
        grid=(n_seq,),
        in_specs=[col(0), col(1), col(2), tab, tab],
        out_specs=[o_spec, o_spec, o_spec],
        out_shape=[shp, shp, shp],
        compiler_params=_cparams(("arbitrary",)),
        name="attn_prep_sample",
    )(proj, proj, proj, cos, sin)


def _top_blocks(gate, n_valid, on_pick):
    lane = lax.broadcasted_iota(jnp.int32, gate.shape, 1)
    g = jnp.where(lane < n_valid, gate, -jnp.inf)
    for r in range(MOBA_TOPK):
        m = jnp.max(g, axis=1, keepdims=True)
        idx = jnp.min(jnp.where(g == m, lane, LANES), axis=1, keepdims=True)
        on_pick(r, idx, m > -jnp.inf)
        g = jnp.where(lane == idx, -jnp.inf, g)


def _attn_prep_t_body(q_ref, k_ref, v_ref, cos_ref, sin_ref, qt_o, ko, vo, kb_o, vt_o, kmo):
    cos, sin = cos_ref[...], sin_ref[...]
    for h in range(N_ATT_HEADS):
        cols = slice(h * HEAD_DIM, (h + 1) * HEAD_DIM)
        qt_o[h] = _rope(q_ref[:, cols], cos, sin).T
        kr = _rope(k_ref[:, cols], cos, sin)
        v = v_ref[:, cols]
        ko[h] = kr
        vo[h] = v
        kb_o[h] = kr.astype(bf16)
        vt_o[h] = jnp.concatenate([v.T.astype(bf16), jnp.ones((V_ROWS - HEAD_DIM, MOBA_BLOCK), bf16)], axis=0)
        kmo[h] = jnp.mean(kr, axis=0, keepdims=True)


def attn_prep_prompt_t(proj, cos, sin):
    t = proj.shape[0]
    assert t % MOBA_BLOCK == 0
    nb = t // MOBA_BLOCK
    h_n = N_ATT_HEADS
    col = lambda cb: pl.BlockSpec((MOBA_BLOCK, ATT_W), lambda i: (i, cb))
    tab = pl.BlockSpec((MOBA_BLOCK, HEAD_DIM), lambda i: (i, 0))
    hm_spec = pl.BlockSpec((h_n, MOBA_BLOCK, HEAD_DIM), lambda i: (0, i, 0))
    t_spec = lambda rows: pl.BlockSpec((h_n, None, rows, MOBA_BLOCK), lambda i: (0, i, 0, 0))
    hm = lambda dt: jax.ShapeDtypeStruct((h_n, t, HEAD_DIM), dt)
    tr = lambda rows, dt: jax.ShapeDtypeStruct((h_n, nb, rows, MOBA_BLOCK), dt)
    qt, k, v, kb, vt, km = pl.pallas_call(
        _attn_prep_t_body,
        grid=(nb,),
        in_specs=[col(0), col(1), col(2), tab, tab],
        out_specs=[t_spec(HEAD_DIM), hm_spec, hm_spec, hm_spec, t_spec(V_ROWS),
                   pl.BlockSpec((h_n, None, 1, HEAD_DIM), lambda i: (0, i, 0, 0))],
        out_shape=[tr(HEAD_DIM, f32), hm(f32), hm(f32), hm(bf16), tr(V_ROWS, bf16),
                   jax.ShapeDtypeStruct((h_n, nb, 1, HEAD_DIM), f32)],
        compiler_params=_cparams(("arbitrary",)),
        name="attn_prep_prompt",
    )(proj, proj, proj, cos, sin)
    return qt, k, v, kb, vt, km.reshape(h_n, nb, HEAD_DIM)


def _attn_prompt_t_body(qt_ref, kb_ref, vt_ref, km_ref, o_ref, m_scr, acc_scr, *, n_blk):
    i = pl.program_id(1)
    blk = MOBA_BLOCK
    qt = qt_ref[...]
    gate = jnp.dot(km_ref[...], qt, precision=HIGHEST, preferred_element_type=f32)
    sub = lax.broadcasted_iota(jnp.int32, gate.shape, 0)
    g = jnp.where(sub < i, gate, -jnp.inf)
    sel = jnp.zeros(gate.shape, f32)
    for _ in range(MOBA_TOPK):
        m = jnp.max(g, axis=0, keepdims=True)
        idx = jnp.min(jnp.where(g == m, sub, LANES), axis=0, keepdims=True)
        pick = sub == idx
        sel = jnp.where(pick, jnp.where(m > -jnp.inf, 1.0, sel), sel)
        g = jnp.where(pick, -jnp.inf, g)
    bias = jnp.where(sel > 0.0, 0.0, NEG)
    q_aug = jnp.concatenate([(qt * (HEAD_DIM ** -0.5 * LOG2_E)).astype(bf16), bias.astype(bf16)], axis=0)

    m_scr[...] = jnp.full(m_scr.shape, NEG, f32)
    acc_scr[...] = jnp.zeros_like(acc_scr)
    lane = lax.broadcasted_iota(jnp.int32, (blk, LANES), 1)

    def blocks(work):
        scores = []
        for st, j_blk, k_extra, mask in work:
            start = pl.multiple_of(j_blk * blk, blk)
            k_aug = jnp.concatenate([kb_ref[pl.ds(start, blk), :], k_extra], axis=1)
            s = jnp.dot(k_aug, q_aug, preferred_element_type=f32)
            scores.append(s if mask is None else jnp.where(mask, s, NEG))
        probs, alphas = [], []
        for (st, _, _, _), s in zip(work, scores):
            m_old = m_scr[st]
            m_new = jnp.maximum(m_old, jnp.max(s, axis=0, keepdims=True))
            alpha = jnp.exp2(m_old - m_new)
            m_scr[st] = m_new
            probs.append(jnp.exp2(s - m_new).astype(bf16))
            alphas.append(alpha)
        pvs = [jnp.dot(vt_ref[j_blk], p, preferred_element_type=f32)
               for (_, j_blk, _, _), p in zip(work, probs)]
        for (st, _, _, _), alpha, pv in zip(work, alphas, pvs):
            acc_scr[st] = alpha * acc_scr[st] + pv

    def past_blocks(it, carry):
        work = []
        for st in range(ATTN_STREAMS):
            j = it * ATTN_STREAMS + st
            work.append((st, jnp.minimum(j, n_blk - 1), jnp.where(lane == j, 1.0, 0.0).astype(bf16), None))
        blocks(work)
        return carry

    lax.fori_loop(0, (i + ATTN_STREAMS - 1) // ATTN_STREAMS, past_blocks, 0)
    key = lax.broadcasted_iota(jnp.int32, (blk, blk), 0)
    qry = lax.broadcasted_iota(jnp.int32, (blk, blk), 1)
    blocks([(0, i, jnp.zeros((blk, LANES), bf16), key <= qry)])
    m_fin = m_scr[0]
    for st in range(1, ATTN_STREAMS):
        m_fin = jnp.maximum(m_fin, m_scr[st])
    acc = jnp.zeros((V_ROWS, blk), f32)
    for st in range(ATTN_STREAMS):
        acc = acc + jnp.exp2(m_scr[st] - m_fin) * acc_scr[st]
    o_ref[...] = (acc[:HEAD_DIM] / acc[HEAD_DIM:HEAD_DIM + 1]).T.astype(o_ref.dtype)


def attn_prompt_t(qt, kb, vt, km):
    h_n, nb = qt.shape[:2]
    t = nb * MOBA_BLOCK
    assert nb + ATTN_STREAMS <= LANES and HEAD_DIM == LANES
    km_pad = jnp.pad(km, ((0, 0), (0, LANES - nb), (0, 0)))
    stat = pltpu.VMEM((ATTN_STREAMS, 1, MOBA_BLOCK), f32)
    return pl.pallas_call(
        functools.partial(_attn_prompt_t_body, n_blk=nb),
        grid=(h_n, nb),
        in_specs=[pl.BlockSpec((None, None, HEAD_DIM, MOBA_BLOCK), lambda h, i: (h, i, 0, 0)),
                  pl.BlockSpec((None, t, HEAD_DIM), lambda h, i: (h, 0, 0)),
                  pl.BlockSpec((None, nb, V_ROWS, MOBA_BLOCK), lambda h, i: (h, 0, 0, 0)),
                  pl.BlockSpec((None, LANES, HEAD_DIM), lambda h, i: (h, 0, 0))],
        out_specs=pl.BlockSpec((MOBA_BLOCK, HEAD_DIM), lambda h, i: (i, h)),
        out_shape=jax.ShapeDtypeStruct((t, h_n * HEAD_DIM), bf16),
        scratch_shapes=[stat, pltpu.VMEM((ATTN_STREAMS, V_ROWS, MOBA_BLOCK), f32)],
        compiler_params=_cparams(("arbitrary", "arbitrary")),
        name="attn_prompt",
    )(qt, kb, vt, km_pad)


def _sample_select_body(pt_ref, *refs, n_blk, ppb, bps):
    n_pg = ppb * bps
    k_refs, q_ref, sel_ref, km_scr = refs[:n_pg], refs[n_pg], refs[n_pg + 1], refs[n_pg + 2]
    j = pl.program_id(1)

    @pl.when(j == 0)
    def _():
        km_scr[...] = jnp.zeros_like(km_scr)

    for bi in range(bps):
        ks = k_refs[bi * ppb][...].sum(axis=1)
        for p in range(1, ppb):
            ks = ks + k_refs[bi * ppb + p][...].sum(axis=1)
        ks = ks * (1.0 / MOBA_BLOCK)
        for h in range(N_ATT_HEADS):
            km_scr[h, pl.ds(j * bps + bi, 1), :] = ks[h:h + 1, :]

    @pl.when(j == n_blk // bps - 1)
    def _():
        lane = lax.broadcasted_iota(jnp.int32, (SEQ_PAD, LANES), 1)
        sel = [jnp.zeros((SEQ_PAD, LANES), jnp.int32)]
        for h in range(N_ATT_HEADS):
            gate = _dot_nt(q_ref[h], km_scr[h], HIGHEST)

            def on_pick(r, idx, has, h=h):
                sel[0] = jnp.where(lane == h * MOBA_TOPK + r, idx, sel[0])

            _top_blocks(gate, n_blk, on_pick)
        sel_ref[...] = sel[0]


def sample_select(cache_k_hm, layer, page_table, q_s):
    n_seq, n_pages = page_table.shape
    page = cache_k_hm.shape[3]
    ppb = MOBA_BLOCK // page
    assert ppb * page == MOBA_BLOCK and n_pages % ppb == 0
    n_blk = n_pages // ppb
    assert MOBA_TOPK <= n_blk <= LANES
    bps = next(c for c in (8, 4, 2, 1) if n_blk % c == 0)
    h_n = N_ATT_HEADS

    def page_spec(p):
        return pl.BlockSpec((None, None, h_n, page, HEAD_DIM),
                            lambda b, j, pt: (layer, pt[b, j * (bps * ppb) + p], 0, 0, 0))

    return pl.pallas_call(
        functools.partial(_sample_select_body, n_blk=n_blk, ppb=ppb, bps=bps),
        grid_spec=pltpu.PrefetchScalarGridSpec(
            num_scalar_prefetch=1, grid=(n_seq, n_blk // bps),
            in_specs=[page_spec(p) for p in range(bps * ppb)]
                     + [pl.BlockSpec((None, h_n, SEQ_PAD, HEAD_DIM), lambda b, j, pt: (b, 0, 0, 0))],
            out_specs=pl.BlockSpec((None, SEQ_PAD, LANES), lambda b, j, pt: (b, 0, 0)),
            scratch_shapes=[pltpu.VMEM((h_n, LANES, HEAD_DIM), f32)]),
        out_shape=jax.ShapeDtypeStruct((n_seq, SEQ_PAD, LANES), jnp.int32),
        compiler_params=_cparams(("arbitrary", "arbitrary")),
        name="sample_select",
    )(page_table, *([cache_k_hm] * (bps * ppb)), q_s)


def _sample_attn_body(pt_ref, sel_ref, ck_ref, cv_ref, q_ref, kn_ref, vn_ref, o_ref,
                      kbuf, vbuf, sem, *, layer, t_new, ppb, page, n_seq):
    b = pl.program_id(0)
    slot = b % 2
    h_n = N_ATT_HEADS
    heads = range(h_n)

    def copies(seq, sl, t):
        out = []
        for h in heads:
            th = t * h_n + h
            for r in range(MOBA_TOPK):
                blk = sel_ref[seq, th * MOBA_TOPK + r]
                for p in range(ppb):
                    pg = pt_ref[seq, blk * ppb + p]
                    dst = pl.ds((r * ppb + p) * page, page)
                    out.append(pltpu.make_async_copy(ck_ref.at[layer, pg, h], kbuf.at[sl, th, dst],
                                                     sem.at[sl, t, 0]))
                    out.append(pltpu.make_async_copy(cv_ref.at[layer, pg, h], vbuf.at[sl, th, dst],
                                                     sem.at[sl, t, 1]))
        return out

    def start_seq(seq, sl):
        for t in range(t_new):
            for cp in copies(seq, sl, t):
                cp.start()

    @pl.when(b == 0)
    def _():
        start_seq(b, slot)

    @pl.when(b + 1 < n_seq)
    def _():
        start_seq(b + 1, 1 - slot)

    o_ref[...] = jnp.zeros_like(o_ref)
    rown = lax.broadcasted_iota(jnp.int32, (SEQ_PAD, 1), 0)
    for t in range(t_new):
        for cp in copies(b, slot, t):
            cp.wait()
        qrow = [q_ref[h, t:t + 1, :] * (HEAD_DIM ** -0.5) for h in heads]
        s = [jnp.sum(kbuf[slot, t * h_n + h] * qrow[h], axis=1, keepdims=True) for h in heads]
        s_new = [jnp.where(rown <= t, jnp.sum(kn_ref[h] * qrow[h], axis=1, keepdims=True), NEG)
                 for h in heads]
        m = [jnp.maximum(jnp.max(s[h], axis=0, keepdims=True), jnp.max(s_new[h], axis=0, keepdims=True))
             for h in heads]
        e = [jnp.exp(s[h] - m[h]) for h in heads]
        e_new = [jnp.exp(s_new[h] - m[h]) for h in heads]
        den = [jnp.sum(e[h], axis=0, keepdims=True) + jnp.sum(e_new[h], axis=0, keepdims=True)
               for h in heads]
        num = [jnp.sum(e[h] * vbuf[slot, t * h_n + h], axis=0, keepdims=True)
               + jnp.sum(e_new[h] * vn_ref[h], axis=0, keepdims=True) for h in heads]
        for h in heads:
            o_ref[t:t + 1, h * HEAD_DIM:(h + 1) * HEAD_DIM] = num[h] / den[h]


def sample_attention(cache_k_hm, cache_v_hm, layer, page_table, sel, q_s, k_s, v_s, t_new):
    n_seq = page_table.shape[0]
    page = cache_k_hm.shape[3]
    ppb = MOBA_BLOCK // page
    h_n = N_ATT_HEADS
    n_sel = MOBA_TOPK * MOBA_BLOCK
    any_spec = pl.BlockSpec(memory_space=pl.ANY)
    seq_spec = pl.BlockSpec((None, h_n, SEQ_PAD, HEAD_DIM), lambda b, pt, sl: (b, 0, 0, 0))
    return pl.pallas_call(
        functools.partial(_sample_attn_body, layer=layer, t_new=t_new, ppb=ppb, page=page, n_seq=n_seq),
        grid_spec=pltpu.PrefetchScalarGridSpec(
            num_scalar_prefetch=2, grid=(n_seq,),
            in_specs=[any_spec, any_spec, seq_spec, seq_spec, seq_spec],
            out_specs=pl.BlockSpec((None, SEQ_PAD, h_n * HEAD_DIM), lambda b, pt, sl: (b, 0, 0)),
            scratch_shapes=[pltpu.VMEM((2, t_new * h_n, n_sel, HEAD_DIM), f32),
                            pltpu.VMEM((2, t_new * h_n, n_sel, HEAD_DIM), f32),
                            pltpu.SemaphoreType.DMA((2, t_new, 2))]),
        out_shape=jax.ShapeDtypeStruct((n_seq, SEQ_PAD, h_n * HEAD_DIM), f32),
        compiler_params=_cparams(("arbitrary",)),
        name="sample_attention",
    )(page_table, sel, cache_k_hm, cache_v_hm, q_s, k_s, v_s)


def _halo_cat(prev_ref, init_ref, x_ref):
    halo = jnp.where(pl.program_id(1) == 0, init_ref[...], prev_ref[...])
    return jnp.concatenate([halo, x_ref[...]], axis=0)


def _halo_specs(n_tiles, tm, halo, width, x_col, init_col):
    assert tm % halo == 0 or n_tiles == 1
    per = tm // halo

    def prev_map(b, i, *r):
        return (jnp.maximum((b * n_tiles + i) * per - 1, 0), x_col(*r))

    return [pl.BlockSpec((halo, width), prev_map),
            pl.BlockSpec((None, halo, width), lambda b, i, *r: (b, 0, init_col(*r))),
            pl.BlockSpec((tm, width), lambda b, i, *r: (b * n_tiles + i, x_col(*r)))]


def _gdn_prep_body(prev_ref, init_ref, x_ref, ab_ref, cw_ref, alog_ref, dtb_ref,
                   q_o, k_o, v_o, gb_o, *, tm, t_valid):
    cat = _halo_cat(prev_ref, init_ref, x_ref)
    w = cw_ref[...]
    acc = x_ref[...] * w[GDN_CONV - 1:GDN_CONV, :]
    for s in range(1, GDN_CONV):
        acc = acc + pltpu.roll(cat, s, axis=0)[SUBLANES:] * w[GDN_CONV - 1 - s:GDN_CONV - s, :]
    y = _silu(acc)
    row = pl.program_id(1) * tm + lax.broadcasted_iota(jnp.int32, (tm, 1), 0)
    valid = row < t_valid
    for h in range(N_GDN_HEADS):
        lo = h * HEAD_DIM
        qh = y[:, lo:lo + HEAD_DIM]
        kh = y[:, GDN_W + lo:GDN_W + lo + HEAD_DIM]
        qn = qh * lax.rsqrt(jnp.sum(qh * qh, axis=-1, keepdims=True) + EPS) * (HEAD_DIM ** -0.5)
        kn = kh * lax.rsqrt(jnp.sum(kh * kh, axis=-1, keepdims=True) + EPS)
        q_o[:, lo:lo + HEAD_DIM] = jnp.where(valid, qn, 0.0)
        k_o[:, lo:lo + HEAD_DIM] = jnp.where(valid, kn, 0.0)
    v_o[...] = jnp.where(valid, y[:, 2 * GDN_W:], 0.0)
    ab = ab_ref[...]
    g = -jnp.exp(alog_ref[...]) * _softplus(ab + dtb_ref[...])
    lane = lax.broadcasted_iota(jnp.int32, ab.shape, 1)
    gb = jnp.where(lane < N_GDN_HEADS, g, jnp.where(lane < 2 * N_GDN_HEADS, _sigmoid(ab), 0.0))
    gb_o[...] = jnp.where(valid, gb, 0.0)


def gdn_prep(proj, init, conv_w, a_log, dt_bias, *, n_seq, t_pad, t_valid, tm_pref=256):
    m = proj.shape[0]
    tm = _tile(t_pad, tm_pref)
    n_tiles = t_pad // tm
    w3 = 3 * GDN_W
    pad = lambda a: jnp.pad(a.reshape(1, -1), ((0, 0), (0, LANES - a.shape[-1])))
    ab_blk = COL_AB // LANES
    out = jax.ShapeDtypeStruct((m, GDN_W), f32)
    row_spec = lambda wd: pl.BlockSpec((tm, wd), lambda b, i: (b * n_tiles + i, 0))
    const = lambda r, c: pl.BlockSpec((r, c), lambda b, i: (0, 0))
    return pl.pallas_call(
        functools.partial(_gdn_prep_body, tm=tm, t_valid=t_valid),
        grid=(n_seq, n_tiles),
        in_specs=_halo_specs(n_tiles, tm, SUBLANES, w3, lambda: COL_QKV_B // w3, lambda: 0)
                 + [pl.BlockSpec((tm, LANES), lambda b, i: (b * n_tiles + i, ab_blk)),
                    const(GDN_CONV, w3), const(1, LANES), const(1, LANES)],
        out_specs=[row_spec(GDN_W), row_spec(GDN_W), row_spec(GDN_W), row_spec(LANES)],
        out_shape=[out, out, out, jax.ShapeDtypeStruct((m, LANES), f32)],
        compiler_params=_cparams(("arbitrary", "arbitrary")),
        name="gdn_prep",
    )(proj, init, proj, proj, conv_w, pad(a_log), pad(dt_bias))


def _gdn_intra_body(q_ref, k_ref, v_ref, gb_ref, qe_ref, o0_ref, m_ref, b_ref, gl_ref, *, rt):
    cs = GDN_CHUNK
    nc = rt // cs
    gb = gb_ref[...]
    r = lax.broadcasted_iota(jnp.int32, (rt, rt), 0)
    c = lax.broadcasted_iota(jnp.int32, (rt, rt), 1)
    in_chunk = c >= r - r % cs
    causal_f = jnp.where(c <= r, jnp.where(in_chunk, 1.0, 0.0), 0.0)
    strict_f = jnp.where(c < r, causal_f, 0.0)
    causal, strict = causal_f > 0.0, strict_f > 0.0
    eye = jnp.where(r == c, 1.0, 0.0)
    gcum = jnp.dot(causal_f, gb, precision=HIGHEST, preferred_element_type=f32)
    gl_rows = jnp.concatenate(
        [jnp.broadcast_to(gcum[(j + 1) * cs - 1:(j + 1) * cs, :], (cs, LANES)) for j in range(nc)], axis=0)
    for j in range(nc):
        gl_ref[j] = jnp.exp(gl_rows[j * cs:j * cs + SUBLANES, :])
    heads = range(N_GDN_HEADS)
    cols = [slice(h * HEAD_DIM, (h + 1) * HEAD_DIM) for h in heads]
    q, k, v = ([ref[:, cl] for cl in cols] for ref in (q_ref, k_ref, v_ref))
    gcol = [gcum[:, h:h + 1] for h in heads]
    bcol = [gb[:, N_GDN_HEADS + h:N_GDN_HEADS + h + 1] for h in heads]

    def decay_of(gc):
        gi = jnp.broadcast_to(gc, (rt, rt))
        grow = jnp.sum(jnp.where(r == c, gi, 0.0), axis=0, keepdims=True)
        return jnp.exp(jnp.where(causal, gi - grow, NEG))

    decay = [decay_of(gcol[h]) for h in heads]
    kb = [k[h].astype(bf16) for h in heads]
    a = [jnp.where(strict, bcol[h] * _dot_nt(kb[h], kb[h]) * decay[h], 0.0) for h in heads]
    def in_blocks(size):
        return (r // size) == (c // size)

    a_in = [jnp.where(in_blocks(GDN_BASE), a[h], 0.0) for h in heads]
    x = [eye - a_in[h] for h in heads]
    p = [_dot3(a_in[h], a_in[h]) for h in heads]
    n_sq = max(1, (GDN_BASE - 1).bit_length() - 1)
    for it in range(n_sq):
        x = [x[h] + _dot3(x[h], p[h]) for h in heads]
        if it + 1 < n_sq:
            p = [_dot3(p[h], p[h]) for h in heads]
    size = GDN_BASE
    while size < cs:
        size *= 2
        a_out = [jnp.where(in_blocks(size), a[h], 0.0) - a_in[h] for h in heads]
        x = [x[h] - _dot3(_dot3(x[h], a_out[h]), x[h]) for h in heads]
        a_in = [a_in[h] + a_out[h] for h in heads]
    eg = [jnp.exp(gcol[h]) for h in heads]
    sol = [_dot3(x[h], jnp.concatenate([v[h] * bcol[h], k[h] * (bcol[h] * eg[h])], axis=1))
           for h in heads]
    qk = [jnp.where(causal, _dot_nt(q[h].astype(bf16), kb[h]) * decay[h], 0.0) for h in heads]
    qs = [_bdot(qk[h], sol[h]) for h in heads]
    for h in heads:
        o0_ref[:, cols[h]] = qs[h][:, :HEAD_DIM]
        qe_ref[:, cols[h]] = q[h] * eg[h] - qs[h][:, HEAD_DIM:]
    for h in heads:
        kt = (k[h] * jnp.exp(gl_rows[:, h:h + 1] - gcol[h])).astype(bf16)
        solb = sol[h].astype(bf16)
        for j in range(nc):
            rows = slice(j * cs, (j + 1) * cs)
            mb = _dot_tn(kt[rows], solb[rows])
            b_ref[j, h] = mb[:, :HEAD_DIM]
            m_ref[j, h] = -mb[:, HEAD_DIM:]


def _gdn_scan_body(qe_ref, o0_ref, m_ref, b_ref, gl_ref, z_ref, nw_ref, s0_ref, o_ref, s_out_ref, s_scr,
                   *, cps, n_steps):
    step = pl.program_id(1)
    cs = GDN_CHUNK

    @pl.when(step == 0)
    def _():
        s_scr[...] = s0_ref[...]

    for j in range(cps):
        rows = slice(j * cs, (j + 1) * cs)
        for h in range(N_GDN_HEADS):
            cols = slice(h * HEAD_DIM, (h + 1) * HEAD_DIM)
            s = s_scr[h]
            sb = s.astype(bf16)
            o = jnp.dot(qe_ref[rows, cols].astype(bf16), sb, preferred_element_type=f32) + o0_ref[rows, cols]
            s_scr[h] = (s * gl_ref[j, 0:1, h:h + 1]
                        + jnp.dot(m_ref[j, h].astype(bf16), sb, preferred_element_type=f32) + b_ref[j, h])
            on = o * lax.rsqrt(jnp.mean(o * o, axis=-1, keepdims=True) + EPS) * nw_ref[...]
            o_ref[rows, cols] = (on * _silu(z_ref[rows, cols])).astype(o_ref.dtype)

    @pl.when(step == n_steps - 1)
    def _():
        s_out_ref[...] = s_scr[...]


def gdn_recurrence(q, k, v, gb, z, z_blk, norm_w, s0, s0_layer, *, n_seq, t_pad):
    m = q.shape[0]
    cs = GDN_CHUNK
    assert t_pad % cs == 0
    rt = _tile(m, GDN_TILE)
    assert rt % cs == 0
    nct = rt // cs
    nc_all = m // cs
    h_n = N_GDN_HEADS
    rows = lambda wd: pl.BlockSpec((rt, wd), lambda i: (i, 0))
    mats = pl.BlockSpec((nct, h_n, HEAD_DIM, HEAD_DIM), lambda i: (i, 0, 0, 0))
    mat_shape = jax.ShapeDtypeStruct((nc_all, h_n, HEAD_DIM, HEAD_DIM), f32)
    qe, o0, mm, bb, gl = pl.pallas_call(
        functools.partial(_gdn_intra_body, rt=rt),
        grid=(m // rt,),
        in_specs=[rows(GDN_W), rows(GDN_W), rows(GDN_W), rows(LANES)],
        out_specs=[rows(GDN_W), rows(GDN_W), mats, mats,
                   pl.BlockSpec((nct, SUBLANES, LANES), lambda i: (i, 0, 0))],
        out_shape=[jax.ShapeDtypeStruct((m, GDN_W), f32), jax.ShapeDtypeStruct((m, GDN_W), f32),
                   mat_shape, mat_shape, jax.ShapeDtypeStruct((nc_all, SUBLANES, LANES), f32)],
        compiler_params=_cparams(("arbitrary",)),
        name="gdn_intra",
    )(q, k, v, gb)

    n_chunks = t_pad // cs
    cps = next(c for c in (4, 2, 1) if n_chunks % c == 0)
    n_steps = n_chunks // cps
    srow = lambda wd, cb=0: pl.BlockSpec((cps * cs, wd), lambda b, s: (b * n_steps + s, cb))
    smat = pl.BlockSpec((cps, h_n, HEAD_DIM, HEAD_DIM), lambda b, s: (b * n_steps + s, 0, 0, 0))
    return pl.pallas_call(
        functools.partial(_gdn_scan_body, cps=cps, n_steps=n_steps),
        grid=(n_seq, n_steps),
        in_specs=[srow(GDN_W), srow(GDN_W), smat, smat,
                  pl.BlockSpec((cps, SUBLANES, LANES), lambda b, s: (b * n_steps + s, 0, 0)),
                  srow(GDN_W, z_blk), pl.BlockSpec((1, HEAD_DIM), lambda b, s: (0, 0)),
                  pl.BlockSpec((None, None, h_n, HEAD_DIM, HEAD_DIM), lambda b, s: (s0_layer, b, 0, 0, 0))],
        out_specs=[srow(GDN_W),
                   pl.BlockSpec((None, h_n, HEAD_DIM, HEAD_DIM), lambda b, s: (b, 0, 0, 0))],
        out_shape=[jax.ShapeDtypeStruct((m, GDN_W), bf16),
                   jax.ShapeDtypeStruct((n_seq, h_n, HEAD_DIM, HEAD_DIM), f32)],
        scratch_shapes=[pltpu.VMEM((h_n, HEAD_DIM, HEAD_DIM), f32)],
        compiler_params=_cparams(("arbitrary", "arbitrary")),
        name="gdn_scan",
    )(qe, o0, mm, bb, gl, z, norm_w.reshape(1, HEAD_DIM), s0)


def _pool_body(prev_ref, init_ref, x_ref, pw_ref, ps_ref, o_ref, *, tm, start):
    cat = _halo_cat(prev_ref, init_ref, x_ref)
    sums, s, span = [], cat, 1
    for _ in POOL_WINDOWS:
        s = s + pltpu.roll(s, span, axis=0)
        span *= 2
        sums.append(s)
    assert span == POOL_HALO
    pos = start + pl.program_id(1) * tm + lax.broadcasted_iota(jnp.int32, (tm, 1), 0)
    x = x_ref[...]
    for g, win in enumerate(POOL_WINDOWS):
        lo = g * POOL_GROUP_W
        cnt = jnp.minimum(win, pos + 1).astype(f32)
        mix = sums[g][POOL_HALO:, lo:lo + POOL_GROUP_W] / cnt - x[:, lo:lo + POOL_GROUP_W]
        out = jnp.dot(mix, pw_ref[g], precision=HIGHEST, preferred_element_type=f32)
        o_ref[:, lo:lo + POOL_GROUP_W] = (out * ps_ref[:, lo:lo + POOL_GROUP_W]).astype(o_ref.dtype)


def pool_mix(proj, init, pool_w, pool_scale, *, n_seq, t_pad, start, tm_pref=512):
    m = proj.shape[0]
    tm = _tile(t_pad, tm_pref)
    n_tiles = t_pad // tm
    return pl.pallas_call(
        functools.partial(_pool_body, tm=tm, start=start),
        grid=(n_seq, n_tiles),
        in_specs=_halo_specs(n_tiles, tm, POOL_HALO, POOL_W, lambda: COL_POOL // POOL_W, lambda: 0)
                 + [pl.BlockSpec(pool_w.shape, lambda b, i: (0, 0, 0)),
                    pl.BlockSpec((1, POOL_W), lambda b, i: (0, 0))],
        out_specs=pl.BlockSpec((tm, POOL_W), lambda b, i: (b * n_tiles + i, 0)),
        out_shape=jax.ShapeDtypeStruct((m, POOL_W), _act_dtype(tm)),
        compiler_params=_cparams(("arbitrary", "arbitrary")),
        name="pool_mix",
    )(proj, init, proj, pool_w, pool_scale.reshape(1, POOL_W))


def _conv_act_body(prev_ref, init_ref, a_ref, u_ref, cw_ref, o_ref):
    cat = _halo_cat(prev_ref, init_ref, a_ref)
    w = cw_ref[...]
    acc = a_ref[...] * w[FFN_CONV - 1:FFN_CONV, :]
    for s in range(1, FFN_CONV):
        acc = acc + pltpu.roll(cat, s, axis=0)[SUBLANES:] * w[FFN_CONV - 1 - s:FFN_CONV - s, :]
    o_ref[...] = (_silu(acc) * u_ref[...]).astype(o_ref.dtype)


def conv_act(a, u, init, conv_w, *, n_seq, t_pad, tm_pref=512, tile_elems=512 * 512):
    m, f = a.shape
    tm = _tile(t_pad, tm_pref)
    tf = _tile(f, max(LANES, tile_elems // tm // LANES * LANES))
    n_tiles = t_pad // tm
    return pl.pallas_call(
        _conv_act_body,
        grid=(n_seq, n_tiles, f // tf),
        in_specs=_halo_specs(n_tiles, tm, SUBLANES, tf, lambda j: j, lambda j: j)
                 + [pl.BlockSpec((tm, tf), lambda b, i, j: (b * n_tiles + i, j)),
                    pl.BlockSpec((FFN_CONV, tf), lambda b, i, j: (0, j))],
        out_specs=pl.BlockSpec((tm, tf), lambda b, i, j: (b * n_tiles + i, j)),
        out_shape=jax.ShapeDtypeStruct((m, f), _act_dtype(tm)),
        compiler_params=_cparams(("arbitrary", "arbitrary", "arbitrary")),
        name="conv_act",
    )(a, init, a, u, conv_w)


def _rope_tables(pos):
    half = HEAD_DIM // 2
    inv = ROPE_THETA ** (-jnp.arange(half, dtype=f32) * 2.0 / HEAD_DIM)
    ang = pos.astype(f32)[:, None] * inv[None, :]
    cos, sin = jnp.cos(ang), jnp.sin(ang)
    return jnp.concatenate([cos, cos], axis=-1), jnp.concatenate([-sin, sin], axis=-1)


def _reorder_w_in(w_in):
    qkv = 3 * ATT_W + 3 * GDN_W
    z_lo = qkv
    ab_lo = z_lo + GDN_W
    pool_lo = ab_lo + 2 * N_GDN_HEADS
    assert w_in.shape[-1] == pool_lo + POOL_W
    zeros = jnp.zeros(w_in.shape[:-1] + (COL_Z - COL_AB - 2 * N_GDN_HEADS,), w_in.dtype)
    w = jnp.concatenate([w_in[..., :qkv], w_in[..., pool_lo:], w_in[..., ab_lo:pool_lo], zeros,
                         w_in[..., z_lo:ab_lo]], axis=-1)
    return w.astype(bf16)


def _tail_rows(buf, x, t_valid, lo, hi, n):
    if t_valid >= n:
        return x[:, t_valid - n:t_valid, lo:hi]
    return jnp.concatenate([buf[:, t_valid:], x[:, :t_valid, lo:hi]], axis=1)


def _front_pad(state, rows):
    return jnp.pad(state, ((0, 0), (rows - state.shape[1], 0), (0, 0)))


def _group_layer(x, mods, layer, lw, *, n_seq, t_pad, t_valid, start, states, attn_fn):
    (g_mix, g_ffn, w_in_r, gdn_conv_w, gdn_a_log, gdn_dt_bias, gdn_norm_w, pool_w, pool_scale,
     w_out, w_gate, w_up, ffn_conv_w, w_down) = lw
    sh1, sc1, gt1, sh2, sc2, gt2 = mods
    s_gdn, s_gdn_layer, gdn_buf, pool_buf, ffn_buf = states
    m = x.shape[0]
    (proj,) = matmul(norm_mod(x, g_mix, sc1, sh1), [w_in_r], layer, name="in_proj")
    proj3 = proj.reshape(n_seq, t_pad, IN_W_PAD)

    o_a, k_new, v_new = attn_fn(proj)

    qn, kn, vc, gb = gdn_prep(proj, _front_pad(gdn_buf, SUBLANES), gdn_conv_w, gdn_a_log, gdn_dt_bias,
                              n_seq=n_seq, t_pad=t_pad, t_valid=t_valid)
    t_gdn = -(-t_pad // GDN_CHUNK) * GDN_CHUNK
    if t_gdn == t_pad:
        o_b, s_new = gdn_recurrence(qn, kn, vc, gb, proj, COL_Z // GDN_W, gdn_norm_w, s_gdn, s_gdn_layer,
                                    n_seq=n_seq, t_pad=t_pad)
    else:
        def chunk_pad(a):
            a = a.reshape(n_seq, t_pad, a.shape[-1])
            return jnp.pad(a, ((0, 0), (0, t_gdn - t_pad), (0, 0))).reshape(n_seq * t_gdn, -1)
        o_b, s_new = gdn_recurrence(chunk_pad(qn), chunk_pad(kn), chunk_pad(vc), chunk_pad(gb),
                                    chunk_pad(proj[:, COL_Z:COL_Z + GDN_W]), 0, gdn_norm_w, s_gdn,
                                    s_gdn_layer, n_seq=n_seq, t_pad=t_gdn)
        o_b = o_b.reshape(n_seq, t_gdn, GDN_W)[:, :t_pad].reshape(m, GDN_W)
    new_gdn_buf = _tail_rows(gdn_buf, proj3, t_valid, COL_QKV_B, COL_QKV_B + 3 * GDN_W, GDN_CONV - 1)

    o_p = pool_mix(proj, _front_pad(pool_buf, POOL_HALO), pool_w, pool_scale,
                   n_seq=n_seq, t_pad=t_pad, start=start)
    new_pool_buf = _tail_rows(pool_buf, proj3, t_valid, COL_POOL, COL_POOL + POOL_W, POOL_BUF)

    (x,) = matmul([o_a, o_b, o_p], [w_out], layer, x=x, gt=gt1, name="out_proj")

    h2 = norm_mod(x, g_ffn, sc2, sh2)
    init_a = _front_pad(ffn_buf, SUBLANES)
    if n_seq == 1 and t_valid == t_pad and t_pad >= SUBLANES:
        act, a_tail = ffn_up_conv(h2, w_gate, w_up, layer, init_a[0], ffn_conv_w)
        new_ffn_buf = a_tail[None, SUBLANES - (FFN_CONV - 1):]
    else:
        a, u = matmul(h2, [w_gate, w_up], layer, name="ffn_up")
        act = conv_act(a, u, init_a, ffn_conv_w, n_seq=n_seq, t_pad=t_pad)
        f = a.shape[1]
        new_ffn_buf = _tail_rows(ffn_buf, a.reshape(n_seq, t_pad, f), t_valid, 0, f, FFN_CONV - 1)
    (x,) = matmul(act, [w_down], layer, x=x, gt=gt2, tm_pref=512, name="ffn_down")
    return x, k_new, v_new, s_new, new_gdn_buf, new_pool_buf, new_ffn_buf


def kernel(x_prompt, x_sample, cache_k, cache_v, page_table, state_gdn, state_gdn_conv, state_pool,
           state_ffn_conv, c_prompt, c_sample, w_ada, b_ada, g_mix, g_ffn, w_in, gdn_conv_w, gdn_a_log,
           gdn_dt_bias, gdn_norm_w, pool_w, pool_scale, w_out, w_gate, w_up, ffn_conv_w, w_down, g_final):
    bp, t_p, d = x_prompt.shape
    bs, t_s, _ = x_sample.shape
    depth = w_ada.shape[0]
    page = cache_k.shape[2]
    past_len = page_table.shape[1] * page
    assert bp == 1 and t_s <= SEQ_PAD and t_p % MOBA_BLOCK == 0 and past_len % MOBA_BLOCK == 0
    assert t_p >= POOL_BUF

    ck = jnp.transpose(cache_k, (0, 1, 3, 2, 4))
    cv = jnp.transpose(cache_v, (0, 1, 3, 2, 4))

    n_c = bp + bs
    mc = -(-n_c // SUBLANES) * SUBLANES
    c_all = jnp.pad(jnp.concatenate([c_prompt, c_sample], axis=0), ((0, mc - n_c), (0, 0)))
    mod = ada_mod(c_all, w_ada, b_ada)
    w_in_r = _reorder_w_in(w_in)

    cos_p, sin_p = _rope_tables(jnp.arange(t_p))
    cos_s, sin_s = _rope_tables(past_len + jnp.arange(SEQ_PAD))

    xp = x_prompt.reshape(bp * t_p, d)
    xs = jnp.pad(x_sample, ((0, 0), (0, SEQ_PAD - t_s), (0, 0))).reshape(bs * SEQ_PAD, d)

    zeros_p = lambda *shape: jnp.zeros((bp,) + shape, f32)
    outs_p, outs_s = [], []
    for l in range(depth):
        lw = (g_mix[l], g_ffn[l], w_in_r, gdn_conv_w[l], gdn_a_log[l], gdn_dt_bias[l], gdn_norm_w[l],
              pool_w[l], pool_scale[l], w_out, w_gate, w_up, ffn_conv_w[l], w_down)
        mods_p = [mod[l, 0:bp, i * d:(i + 1) * d] for i in range(6)]
        mods_s = [jnp.repeat(mod[l, bp:n_c, i * d:(i + 1) * d], SEQ_PAD, axis=0) for i in range(6)]

        def attn_p(proj):
            qt, k, v, kb, vt, km = attn_prep_prompt_t(proj, cos_p, sin_p)
            return attn_prompt_t(qt, kb, vt, km), k, v

        def attn_s(proj, l=l):
            q, k, v = attn_prep_sample(proj, cos_s, sin_s, bs)
            sel = sample_select(ck, l, page_table, q)
            sel = sel[:, :t_s, :N_ATT_HEADS * MOBA_TOPK].reshape(bs, -1)
            o = sample_attention(ck, cv, l, page_table, sel, q, k, v, t_s)
            return o.reshape(bs * SEQ_PAD, ATT_W), k[:, :, :t_s], v[:, :, :t_s]

        xp, *st_p = _group_layer(
            xp, mods_p, l, lw, n_seq=bp, t_pad=t_p, t_valid=t_p, start=0, attn_fn=attn_p,
            states=(zeros_p(N_GDN_HEADS, HEAD_DIM, HEAD_DIM)[None], 0, zeros_p(GDN_CONV - 1, 3 * GDN_W),
                    zeros_p(POOL_BUF, POOL_W), zeros_p(FFN_CONV - 1, w_gate.shape[-1])))
        xs, *st_s = _group_layer(
            xs, mods_s, l, lw, n_seq=bs, t_pad=SEQ_PAD, t_valid=t_s, start=past_len, attn_fn=attn_s,
            states=(state_gdn, l, state_gdn_conv[l], state_pool[l], state_ffn_conv[l]))
        outs_p.append(st_p)
        outs_s.append(st_s)

    y_prompt = rms_norm(xp, g_final).reshape(bp, t_p, d)
    y_sample = rms_norm(xs, g_final).reshape(bs, SEQ_PAD, d)[:, :t_s]
    kp, vp, gp, gcp, pp, fp = [jnp.stack(t) for t in zip(*outs_p)]
    ks, vs, gs, gcs, ps, fs = [jnp.stack(t) for t in zip(*outs_s)]
    k_prompt = jnp.transpose(kp, (0, 2, 1, 3))[:, None]
    v_prompt = jnp.transpose(vp, (0, 2, 1, 3))[:, None]
    k_sample = jnp.transpose(ks, (0, 1, 3, 2, 4))
    v_sample = jnp.transpose(vs, (0, 1, 3, 2, 4))
    return (y_prompt, y_sample, k_prompt, v_prompt, k_sample, v_sample, gp, gs, gcp, gcs, pp, ps, fp, fs)
```

```python
import functools

import jax
import jax.numpy as jnp
from jax import lax
from jax.experimental import pallas as pl
from jax.experimental.pallas import tpu as pltpu

f32 = jnp.float32
bf16 = jnp.bfloat16
HIGHEST = lax.Precision.HIGHEST

LANES = 128
SUBLANES = 8
VMEM_LIMIT_BYTES = 56 * 1024 * 1024

HEAD_DIM = 128
N_ATT_HEADS = 6
N_GDN_HEADS = 6
ATT_W = N_ATT_HEADS * HEAD_DIM
GDN_W = N_GDN_HEADS * HEAD_DIM
POOL_WINDOWS = (2, 4, 8, 16)
POOL_GROUP_W = 128
POOL_W = len(POOL_WINDOWS) * POOL_GROUP_W
POOL_BUF = 15
POOL_HALO = 16
MOBA_BLOCK = 256
MOBA_TOPK = 3
ATTN_STREAMS = 8
V_ROWS = 128 + 16
GDN_CHUNK = 64
GDN_BASE = 16
GDN_TILE = 128
GDN_CONV = 4
FFN_CONV = 3
ROPE_THETA = 10000.0
EPS = 1e-6
SEQ_PAD = 8
NEG = -1e30
LOG2_E = 1.4426950408889634

COL_QKV_A = 0
COL_QKV_B = 3 * ATT_W
COL_POOL = COL_QKV_B + 3 * GDN_W
COL_AB = COL_POOL + POOL_W
COL_Z = COL_AB + 256
IN_W_PAD = COL_Z + GDN_W


def _cparams(sem):
    return pltpu.CompilerParams(dimension_semantics=sem, vmem_limit_bytes=VMEM_LIMIT_BYTES)


def _tile(n, pref):
    t = min(n, pref)
    while n % t:
        t -= SUBLANES
    assert t > 0 and n % t == 0
    return t


def _act_dtype(tm):
    return bf16 if tm % (2 * SUBLANES) == 0 else f32


def _sigmoid(x):
    return 1.0 / (1.0 + jnp.exp(-x))


def _silu(x):
    return x * _sigmoid(x)


def _softplus(x):
    return jnp.maximum(x, 0.0) + jnp.log(1.0 + jnp.exp(-jnp.abs(x)))


def _dot_nt(a, b, precision=None):
    return lax.dot_general(a, b, (((1,), (1,)), ((), ())), precision=precision,
                           preferred_element_type=f32)


def _dot_tn(a, b, precision=None):
    return lax.dot_general(a, b, (((0,), (0,)), ((), ())), precision=precision,
                           preferred_element_type=f32)


def _bdot(a, b):
    return jnp.dot(a.astype(bf16), b.astype(bf16), preferred_element_type=f32)


def _split_bf16(a):
    hi = a.astype(bf16)
    return hi, (a - hi.astype(f32)).astype(bf16)


def _dot3(a, b):
    (ah, al), (bh, bl) = _split_bf16(a), _split_bf16(b)
    d = functools.partial(jnp.dot, preferred_element_type=f32)
    return d(ah, bh) + (d(ah, bl) + d(al, bh))


def _mod_spec(mod, tm, ncols):
    if mod.shape[0] == 1:
        return pl.BlockSpec((1, ncols), lambda i, *_: (0, 0))
    return pl.BlockSpec((tm, ncols), lambda i, *_: (i, 0))


def _ada_body(c_ref, w_ref, b_ref, o_ref):
    s = _silu(c_ref[...])
    o_ref[...] = jnp.dot(s.astype(bf16), w_ref[...].astype(bf16),
                         preferred_element_type=f32) + b_ref[...]


def ada_mod(c_all, w_ada, b_ada):
    mc, d = c_all.shape
    depth, _, n = w_ada.shape
    tn = _tile(n, 1024)
    return pl.pallas_call(
        _ada_body,
        grid=(depth, n // tn),
        in_specs=[pl.BlockSpec((mc, d), lambda l, j: (0, 0)),
                  pl.BlockSpec((None, d, tn), lambda l, j: (l, 0, j)),
                  pl.BlockSpec((None, 1, tn), lambda l, j: (l, 0, j))],
        out_specs=pl.BlockSpec((None, mc, tn), lambda l, j: (l, 0, j)),
        out_shape=jax.ShapeDtypeStruct((depth, mc, n), f32),
        compiler_params=_cparams(("arbitrary", "arbitrary")),
        name="ada_mod",
    )(c_all, w_ada, b_ada.reshape(depth, 1, n))


def _norm_mod_body(x_ref, g_ref, sc_ref, sh_ref, o_ref):
    x = x_ref[...]
    y = x * lax.rsqrt(jnp.mean(x * x, axis=-1, keepdims=True) + EPS)
    o_ref[...] = ((y * g_ref[...]) * (1.0 + sc_ref[...]) + sh_ref[...]).astype(o_ref.dtype)


def norm_mod(x, g, sc, sh, *, tm_pref=512, name="norm_mod"):
    m, d = x.shape
    tm = _tile(m, tm_pref)
    return pl.pallas_call(
        _norm_mod_body,
        grid=(m // tm,),
        in_specs=[pl.BlockSpec((tm, d), lambda i: (i, 0)), pl.BlockSpec((1, d), lambda i: (0, 0)),
                  _mod_spec(sc, tm, d), _mod_spec(sh, tm, d)],
        out_specs=pl.BlockSpec((tm, d), lambda i: (i, 0)),
        out_shape=jax.ShapeDtypeStruct((m, d), bf16),
        compiler_params=_cparams(("arbitrary",)),
        name=name,
    )(x, g.reshape(1, d), sc, sh)


def _mm_body(*refs, k_sizes, n_w, residual, cast_w):
    n_a = len(k_sizes)
    a_refs, rest = refs[:n_a], refs[n_a:]
    w_refs, rest = rest[:n_w], rest[n_w:]
    if residual:
        (x_ref, gt_ref), rest = rest[:2], rest[2:]
    o_refs, w_scrs = rest[:n_w], rest[n_w:]

    if cast_w:
        @pl.when(pl.program_id(1) == 0)
        def _():
            for w_ref, w_scr in zip(w_refs, w_scrs):
                w_scr[...] = w_ref[...].astype(bf16)

    a_parts = [a_ref[...].astype(bf16) for a_ref in a_refs]
    for n_i, o_ref in enumerate(o_refs):
        w_src = w_scrs[n_i] if cast_w else w_refs[n_i]
        acc, k0 = None, 0
        for a, k_sz in zip(a_parts, k_sizes):
            part = jnp.dot(a, w_src[k0:k0 + k_sz, :], preferred_element_type=f32)
            acc = part if acc is None else acc + part
            k0 += k_sz
        o_ref[...] = x_ref[...] + gt_ref[...] * acc if residual else acc


def matmul(a, ws, layer, *, x=None, gt=None, tm_pref=1024, tn_pref=512, name="mm"):
    a_list = list(a) if isinstance(a, (list, tuple)) else [a]
    m = a_list[0].shape[0]
    k_sizes = tuple(p.shape[1] for p in a_list)
    kdim = sum(k_sizes)
    n = ws[0].shape[2]
    assert ws[0].shape[1] == kdim
    tm, tn = _tile(m, tm_pref), _tile(n, tn_pref)
    n_w = len(ws)
    residual = x is not None
    assert not residual or n_w == 1
    cast_w = ws[0].dtype != bf16
    in_specs = [pl.BlockSpec((tm, k_sz), lambda j, i: (i, 0)) for k_sz in k_sizes]
    in_specs += [pl.BlockSpec((None, kdim, tn), lambda j, i: (layer, 0, j)) for _ in ws]
    args = [*a_list, *ws]
    if residual:
        gt_spec = (pl.BlockSpec((1, tn), lambda j, i: (0, j)) if gt.shape[0] == 1
                   else pl.BlockSpec((tm, tn), lambda j, i: (i, j)))
        in_specs += [pl.BlockSpec((tm, tn), lambda j, i: (i, j)), gt_spec]
        args += [x, gt]
    outs = pl.pallas_call(
        functools.partial(_mm_body, k_sizes=k_sizes, n_w=n_w, residual=residual, cast_w=cast_w),
        grid=(n // tn, m // tm),
        in_specs=in_specs,
        out_specs=[pl.BlockSpec((tm, tn), lambda j, i: (i, j)) for _ in ws],
        out_shape=[jax.ShapeDtypeStruct((m, n), f32) for _ in ws],
        scratch_shapes=[pltpu.VMEM((kdim, tn), bf16) for _ in ws] if cast_w else [],
        compiler_params=_cparams(("arbitrary", "arbitrary")),
        name=name,
    )(*args)
    return outs


def _ffn_up_conv_body(h_ref, wg_ref, wu_ref, init_ref, cw_ref, act_ref, tail_ref,
                      wg_scr, wu_scr, prev_scr, *, n_row_tiles):
    i = pl.program_id(1)

    @pl.when(i == 0)
    def _():
        wg_scr[...] = wg_ref[...].astype(bf16)
        wu_scr[...] = wu_ref[...].astype(bf16)
        prev_scr[...] = init_ref[...]

    h = h_ref[...]
    a = jnp.dot(h, wg_scr[...], preferred_element_type=f32)
    u = jnp.dot(h, wu_scr[...], preferred_element_type=f32)
    cat = jnp.concatenate([prev_scr[...], a], axis=0)
    w = cw_ref[...]
    acc = a * w[FFN_CONV - 1:FFN_CONV, :]
    for s in range(1, FFN_CONV):
        acc = acc + pltpu.roll(cat, s, axis=0)[SUBLANES:] * w[FFN_CONV - 1 - s:FFN_CONV - s, :]
    act_ref[...] = (_silu(acc) * u).astype(act_ref.dtype)
    last = a[a.shape[0] - SUBLANES:, :]
    prev_scr[...] = last

    @pl.when(i == n_row_tiles - 1)
    def _():
        tail_ref[...] = last


def ffn_up_conv(h, w_gate, w_up, layer, init, conv_w, *, tm_pref=1024, tn_pref=512):
    m, d = h.shape
    f = w_gate.shape[2]
    tm, tn = _tile(m, tm_pref), _tile(f, tn_pref)
    n_row_tiles = m // tm
    w_spec = pl.BlockSpec((None, d, tn), lambda j, i: (layer, 0, j))
    return pl.pallas_call(
        functools.partial(_ffn_up_conv_body, n_row_tiles=n_row_tiles),
        grid=(f // tn, n_row_tiles),
        in_specs=[pl.BlockSpec((tm, d), lambda j, i: (i, 0)), w_spec, w_spec,
                  pl.BlockSpec((SUBLANES, tn), lambda j, i: (0, j)),
                  pl.BlockSpec((FFN_CONV, tn), lambda j, i: (0, j))],
        out_specs=[pl.BlockSpec((tm, tn), lambda j, i: (i, j)),
                   pl.BlockSpec((SUBLANES, tn), lambda j, i: (0, j))],
        out_shape=[jax.ShapeDtypeStruct((m, f), bf16), jax.ShapeDtypeStruct((SUBLANES, f), f32)],
        scratch_shapes=[pltpu.VMEM((d, tn), bf16), pltpu.VMEM((d, tn), bf16),
                        pltpu.VMEM((SUBLANES, tn), f32)],
        compiler_params=_cparams(("arbitrary", "arbitrary")),
        name="ffn_up_conv",
    )(h, w_gate, w_up, init, conv_w)


def _rms_body(x_ref, g_ref, o_ref):
    x = x_ref[...]
    o_ref[...] = x * lax.rsqrt(jnp.mean(x * x, axis=-1, keepdims=True) + EPS) * g_ref[...]


def rms_norm(x, g):
    m, d = x.shape
    tm = _tile(m, 512)
    return pl.pallas_call(
        _rms_body,
        grid=(m // tm,),
        in_specs=[pl.BlockSpec((tm, d), lambda i: (i, 0)), pl.BlockSpec((1, d), lambda i: (0, 0))],
        out_specs=pl.BlockSpec((tm, d), lambda i: (i, 0)),
        out_shape=jax.ShapeDtypeStruct((m, d), f32),
        compiler_params=_cparams(("arbitrary",)),
        name="final_norm",
    )(x, g.reshape(1, d))


def _rope(x, cos, sin_signed):
    return x * cos + pltpu.roll(x, HEAD_DIM // 2, axis=1) * sin_signed


def _attn_prep_body(q_ref, k_ref, v_ref, cos_ref, sin_ref, qo, ko, vo):
    cos, sin = cos_ref[...], sin_ref[...]
    for h in range(N_ATT_HEADS):
        cols = slice(h * HEAD_DIM, (h + 1) * HEAD_DIM)
        qo[h] = _rope(q_ref[:, cols], cos, sin)
        ko[h] = _rope(k_ref[:, cols], cos, sin)
        vo[h] = v_ref[:, cols]


def attn_prep_sample(proj, cos, sin, n_seq):
    h_n = N_ATT_HEADS
    shp = jax.ShapeDtypeStruct((n_seq, h_n, SEQ_PAD, HEAD_DIM), f32)
    col = lambda cb: pl.BlockSpec((SEQ_PAD, ATT_W), lambda b: (b, cb))
    o_spec = pl.BlockSpec((None, h_n, SEQ_PAD, HEAD_DIM), lambda b: (b, 0, 0, 0))
    tab = pl.BlockSpec((SEQ_PAD, HEAD_DIM), lambda b: (0, 0))
    return pl.pallas_call(
        _attn_prep_body,
        grid=(n_seq,),
        in_specs=[col(0), col(1), col(2), tab, tab],
        out_specs=[o_spec, o_spec, o_spec],
        out_shape=[shp, shp, shp],
        compiler_params=_cparams(("arbitrary",)),
        name="attn_prep_sample",
    )(proj, proj, proj, cos, sin)


def _top_blocks(gate, n_valid, on_pick):
    lane = lax.broadcasted_iota(jnp.int32, gate.shape, 1)
    g = jnp.where(lane < n_valid, gate, -jnp.inf)
    for r in range(MOBA_TOPK):
        m = jnp.max(g, axis=1, keepdims=True)
        idx = jnp.min(jnp.where(g == m, lane, LANES), axis=1, keepdims=True)
        on_pick(r, idx, m > -jnp.inf)
        g = jnp.where(lane == idx, -jnp.inf, g)


def _attn_prep_t_body(q_ref, k_ref, v_ref, cos_ref, sin_ref, qt_o, ko, vo, kb_o, vt_o, kmo):
    cos, sin = cos_ref[...], sin_ref[...]
    for h in range(N_ATT_HEADS):
        cols = slice(h * HEAD_DIM, (h + 1) * HEAD_DIM)
        qt_o[h] = _rope(q_ref[:, cols], cos, sin).T
        kr = _rope(k_ref[:, cols], cos, sin)
        v = v_ref[:, cols]
        ko[h] = kr
        vo[h] = v
        kb_o[h] = kr.astype(bf16)
        vt_o[h] = jnp.concatenate([v.T.astype(bf16), jnp.ones((V_ROWS - HEAD_DIM, MOBA_BLOCK), bf16)], axis=0)
        kmo[h] = jnp.mean(kr, axis=0, keepdims=True)


def attn_prep_prompt_t(proj, cos, sin):
    t = proj.shape[0]
    assert t % MOBA_BLOCK == 0
    nb = t // MOBA_BLOCK
    h_n = N_ATT_HEADS
    col = lambda cb: pl.BlockSpec((MOBA_BLOCK, ATT_W), lambda i: (i, cb))
    tab = pl.BlockSpec((MOBA_BLOCK, HEAD_DIM), lambda i: (i, 0))
    hm_spec = pl.BlockSpec((h_n, MOBA_BLOCK, HEAD_DIM), lambda i: (0, i, 0))
    t_spec = lambda rows: pl.BlockSpec((h_n, None, rows, MOBA_BLOCK), lambda i: (0, i, 0, 0))
    hm = lambda dt: jax.ShapeDtypeStruct((h_n, t, HEAD_DIM), dt)
    tr = lambda rows, dt: jax.ShapeDtypeStruct((h_n, nb, rows, MOBA_BLOCK), dt)
    qt, k, v, kb, vt, km = pl.pallas_call(
        _attn_prep_t_body,
        grid=(nb,),
        in_specs=[col(0), col(1), col(2), tab, tab],
        out_specs=[t_spec(HEAD_DIM), hm_spec, hm_spec, hm_spec, t_spec(V_ROWS),
                   pl.BlockSpec((h_n, None, 1, HEAD_DIM), lambda i: (0, i, 0, 0))],
        out_shape=[tr(HEAD_DIM, f32), hm(f32), hm(f32), hm(bf16), tr(V_ROWS, bf16),
                   jax.ShapeDtypeStruct((h_n, nb, 1, HEAD_DIM), f32)],
        compiler_params=_cparams(("arbitrary",)),
        name="attn_prep_prompt",
    )(proj, proj, proj, cos, sin)
    return qt, k, v, kb, vt, km.reshape(h_n, nb, HEAD_DIM)


def _attn_prompt_t_body(qt_ref, kb_ref, vt_ref, km_ref, o_ref, m_scr, acc_scr, *, n_blk):
    i = pl.program_id(1)
    blk = MOBA_BLOCK
    qt = qt_ref[...]
    gate = jnp.dot(km_ref[...], qt, precision=HIGHEST, preferred_element_type=f32)
    sub = lax.broadcasted_iota(jnp.int32, gate.shape, 0)
    g = jnp.where(sub < i, gate, -jnp.inf)
    sel = jnp.zeros(gate.shape, f32)
    for _ in range(MOBA_TOPK):
        m = jnp.max(g, axis=0, keepdims=True)
        idx = jnp.min(jnp.where(g == m, sub, LANES), axis=0, keepdims=True)
        pick = sub == idx
        sel = jnp.where(pick, jnp.where(m > -jnp.inf, 1.0, sel), sel)
        g = jnp.where(pick, -jnp.inf, g)
    bias = jnp.where(sel > 0.0, 0.0, NEG)
    q_aug = jnp.concatenate([(qt * (HEAD_DIM ** -0.5 * LOG2_E)).astype(bf16), bias.astype(bf16)], axis=0)

    m_scr[...] = jnp.full(m_scr.shape, NEG, f32)
    acc_scr[...] = jnp.zeros_like(acc_scr)
    lane = lax.broadcasted_iota(jnp.int32, (blk, LANES), 1)

    def blocks(work):
        scores = []
        for st, j_blk, k_extra, mask in work:
            start = pl.multiple_of(j_blk * blk, blk)
            k_aug = jnp.concatenate([kb_ref[pl.ds(start, blk), :], k_extra], axis=1)
            s = jnp.dot(k_aug, q_aug, preferred_element_type=f32)
            scores.append(s if mask is None else jnp.where(mask, s, NEG))
        probs, alphas = [], []
        for (st, _, _, _), s in zip(work, scores):
            m_old = m_scr[st]
            m_new = jnp.maximum(m_old, jnp.max(s, axis=0, keepdims=True))
            alpha = jnp.exp2(m_old - m_new)
            m_scr[st] = m_new
            probs.append(jnp.exp2(s - m_new).astype(bf16))
            alphas.append(alpha)
        pvs = [jnp.dot(vt_ref[j_blk], p, preferred_element_type=f32)
               for (_, j_blk, _, _), p in zip(work, probs)]
        for (st, _, _, _), alpha, pv in zip(work, alphas, pvs):
            acc_scr[st] = alpha * acc_scr[st] + pv

    def past_blocks(it, carry):
        work = []
        for st in range(ATTN_STREAMS):
            j = it * ATTN_STREAMS + st
            work.append((st, jnp.minimum(j, n_blk - 1), jnp.where(lane == j, 1.0, 0.0).astype(bf16), None))
        blocks(work)
        return carry

    lax.fori_loop(0, (i + ATTN_STREAMS - 1) // ATTN_STREAMS, past_blocks, 0)
    key = lax.broadcasted_iota(jnp.int32, (blk, blk), 0)
    qry = lax.broadcasted_iota(jnp.int32, (blk, blk), 1)
    blocks([(0, i, jnp.zeros((blk, LANES), bf16), key <= qry)])
    m_fin = m_scr[0]
    for st in range(1, ATTN_STREAMS):
        m_fin = jnp.maximum(m_fin, m_scr[st])
    acc = jnp.zeros((V_ROWS, blk), f32)
    for st in range(ATTN_STREAMS):
        acc = acc + jnp.exp2(m_scr[st] - m_fin) * acc_scr[st]
    o_ref[...] = (acc[:HEAD_DIM] / acc[HEAD_DIM:HEAD_DIM + 1]).T.astype(o_ref.dtype)


def attn_prompt_t(qt, kb, vt, km):
    h_n, nb = qt.shape[:2]
    t = nb * MOBA_BLOCK
    assert nb + ATTN_STREAMS <= LANES and HEAD_DIM == LANES
    km_pad = jnp.pad(km, ((0, 0), (0, LANES - nb), (0, 0)))
    stat = pltpu.VMEM((ATTN_STREAMS, 1, MOBA_BLOCK), f32)
    return pl.pallas_call(
        functools.partial(_attn_prompt_t_body, n_blk=nb),
        grid=(h_n, nb),
        in_specs=[pl.BlockSpec((None, None, HEAD_DIM, MOBA_BLOCK), lambda h, i: (h, i, 0, 0)),
                  pl.BlockSpec((None, t, HEAD_DIM), lambda h, i: (h, 0, 0)),
                  pl.BlockSpec((None, nb, V_ROWS, MOBA_BLOCK), lambda h, i: (h, 0, 0, 0)),
                  pl.BlockSpec((None, LANES, HEAD_DIM), lambda h, i: (h, 0, 0))],
        out_specs=pl.BlockSpec((MOBA_BLOCK, HEAD_DIM), lambda h, i: (i, h)),
        out_shape=jax.ShapeDtypeStruct((t, h_n * HEAD_DIM), bf16),
        scratch_shapes=[stat, pltpu.VMEM((ATTN_STREAMS, V_ROWS, MOBA_BLOCK), f32)],
        compiler_params=_cparams(("arbitrary", "arbitrary")),
        name="attn_prompt",
    )(qt, kb, vt, km_pad)


def _sample_select_body(pt_ref, *refs, n_blk, ppb, bps):
    n_pg = ppb * bps
    k_refs, q_ref, sel_ref, km_scr = refs[:n_pg], refs[n_pg], refs[n_pg + 1], refs[n_pg + 2]
    j = pl.program_id(1)

    @pl.when(j == 0)
    def _():
        km_scr[...] = jnp.zeros_like(km_scr)

    for bi in range(bps):
        ks = k_refs[bi * ppb][...].sum(axis=1)
        for p in range(1, ppb):
            ks = ks + k_refs[bi * ppb + p][...].sum(axis=1)
        ks = ks * (1.0 / MOBA_BLOCK)
        for h in range(N_ATT_HEADS):
            km_scr[h, pl.ds(j * bps + bi, 1), :] = ks[h:h + 1, :]

    @pl.when(j == n_blk // bps - 1)
    def _():
        lane = lax.broadcasted_iota(jnp.int32, (SEQ_PAD, LANES), 1)
        sel = [jnp.zeros((SEQ_PAD, LANES), jnp.int32)]
        for h in range(N_ATT_HEADS):
            gate = _dot_nt(q_ref[h], km_scr[h], HIGHEST)

            def on_pick(r, idx, has, h=h):
                sel[0] = jnp.where(lane == h * MOBA_TOPK + r, idx, sel[0])

            _top_blocks(gate, n_blk, on_pick)
        sel_ref[...] = sel[0]


def sample_select(cache_k_hm, layer, page_table, q_s):
    n_seq, n_pages = page_table.shape
    page = cache_k_hm.shape[3]
    ppb = MOBA_BLOCK // page
    assert ppb * page == MOBA_BLOCK and n_pages % ppb == 0
    n_blk = n_pages // ppb
    assert MOBA_TOPK <= n_blk <= LANES
    bps = next(c for c in (16, 8, 4, 2, 1) if n_blk % c == 0)
    h_n = N_ATT_HEADS

    def page_spec(p):
        return pl.BlockSpec((None, None, h_n, page, HEAD_DIM),
                            lambda b, j, pt: (layer, pt[b, j * (bps * ppb) + p], 0, 0, 0))

    return pl.pallas_call(
        functools.partial(_sample_select_body, n_blk=n_blk, ppb=ppb, bps=bps),
        grid_spec=pltpu.PrefetchScalarGridSpec(
            num_scalar_prefetch=1, grid=(n_seq, n_blk // bps),
            in_specs=[page_spec(p) for p in range(bps * ppb)]
                     + [pl.BlockSpec((None, h_n, SEQ_PAD, HEAD_DIM), lambda b, j, pt: (b, 0, 0, 0))],
            out_specs=pl.BlockSpec((None, SEQ_PAD, LANES), lambda b, j, pt: (b, 0, 0)),
            scratch_shapes=[pltpu.VMEM((h_n, LANES, HEAD_DIM), f32)]),
        out_shape=jax.ShapeDtypeStruct((n_seq, SEQ_PAD, LANES), jnp.int32),
        compiler_params=_cparams(("arbitrary", "arbitrary")),
        name="sample_select",
    )(page_table, *([cache_k_hm] * (bps * ppb)), q_s)


def _sample_attn_body(pt_ref, sel_ref, ck_ref, cv_ref, q_ref, kn_ref, vn_ref, o_ref,
                      kbuf, vbuf, sem, *, layer, t_new, ppb, page, n_seq):
    b = pl.program_id(0)
    slot = b % 2
    h_n = N_ATT_HEADS
    heads = range(h_n)

    def copies(seq, sl, t):
        out = []
        for h in heads:
            th = t * h_n + h
            for r in range(MOBA_TOPK):
                blk = sel_ref[seq, th * MOBA_TOPK + r]
                for p in range(ppb):
                    pg = pt_ref[seq, blk * ppb + p]
                    dst = pl.ds((r * ppb + p) * page, page)
                    out.append(pltpu.make_async_copy(ck_ref.at[layer, pg, h], kbuf.at[sl, th, dst],
                                                     sem.at[sl, t, 0]))
                    out.append(pltpu.make_async_copy(cv_ref.at[layer, pg, h], vbuf.at[sl, th, dst],
                                                     sem.at[sl, t, 1]))
        return out

    def start_seq(seq, sl):
        for t in range(t_new):
            for cp in copies(seq, sl, t):
                cp.start()

    @pl.when(b == 0)
    def _():
        start_seq(b, slot)

    @pl.when(b + 1 < n_seq)
    def _():
        start_seq(b + 1, 1 - slot)

    o_ref[...] = jnp.zeros_like(o_ref)
    rown = lax.broadcasted_iota(jnp.int32, (SEQ_PAD, 1), 0)
    for t in range(t_new):
        for cp in copies(b, slot, t):
            cp.wait()
        qrow = [q_ref[h, t:t + 1, :] * (HEAD_DIM ** -0.5) for h in heads]
        s = [jnp.sum(kbuf[slot, t * h_n + h] * qrow[h], axis=1, keepdims=True) for h in heads]
        s_new = [jnp.where(rown <= t, jnp.sum(kn_ref[h] * qrow[h], axis=1, keepdims=True), NEG)
                 for h in heads]
        m = [jnp.maximum(jnp.max(s[h], axis=0, keepdims=True), jnp.max(s_new[h], axis=0, keepdims=True))
             for h in heads]
        e = [jnp.exp(s[h] - m[h]) for h in heads]
        e_new = [jnp.exp(s_new[h] - m[h]) for h in heads]
        den = [jnp.sum(e[h], axis=0, keepdims=True) + jnp.sum(e_new[h], axis=0, keepdims=True)
               for h in heads]
        num = [jnp.sum(e[h] * vbuf[slot, t * h_n + h], axis=0, keepdims=True)
               + jnp.sum(e_new[h] * vn_ref[h], axis=0, keepdims=True) for h in heads]
        for h in heads:
            o_ref[t:t + 1, h * HEAD_DIM:(h + 1) * HEAD_DIM] = num[h] / den[h]


def sample_attention(cache_k_hm, cache_v_hm, layer, page_table, sel, q_s, k_s, v_s, t_new):
    n_seq = page_table.shape[0]
    page = cache_k_hm.shape[3]
    ppb = MOBA_BLOCK // page
    h_n = N_ATT_HEADS
    n_sel = MOBA_TOPK * MOBA_BLOCK
    any_spec = pl.BlockSpec(memory_space=pl.ANY)
    seq_spec = pl.BlockSpec((None, h_n, SEQ_PAD, HEAD_DIM), lambda b, pt, sl: (b, 0, 0, 0))
    return pl.pallas_call(
        functools.partial(_sample_attn_body, layer=layer, t_new=t_new, ppb=ppb, page=page, n_seq=n_seq),
        grid_spec=pltpu.PrefetchScalarGridSpec(
            num_scalar_prefetch=2, grid=(n_seq,),
            in_specs=[any_spec, any_spec, seq_spec, seq_spec, seq_spec],
            out_specs=pl.BlockSpec((None, SEQ_PAD, h_n * HEAD_DIM), lambda b, pt, sl: (b, 0, 0)),
            scratch_shapes=[pltpu.VMEM((2, t_new * h_n, n_sel, HEAD_DIM), f32),
                            pltpu.VMEM((2, t_new * h_n, n_sel, HEAD_DIM), f32),
                            pltpu.SemaphoreType.DMA((2, t_new, 2))]),
        out_shape=jax.ShapeDtypeStruct((n_seq, SEQ_PAD, h_n * HEAD_DIM), f32),
        compiler_params=_cparams(("arbitrary",)),
        name="sample_attention",
    )(page_table, sel, cache_k_hm, cache_v_hm, q_s, k_s, v_s)


def _halo_cat(prev_ref, init_ref, x_ref):
    halo = jnp.where(pl.program_id(1) == 0, init_ref[...], prev_ref[...])
    return jnp.concatenate([halo, x_ref[...]], axis=0)


def _halo_specs(n_tiles, tm, halo, width, x_col, init_col):
    assert tm % halo == 0 or n_tiles == 1
    per = tm // halo

    def prev_map(b, i, *r):
        return (jnp.maximum((b * n_tiles + i) * per - 1, 0), x_col(*r))

    return [pl.BlockSpec((halo, width), prev_map),
            pl.BlockSpec((None, halo, width), lambda b, i, *r: (b, 0, init_col(*r))),
            pl.BlockSpec((tm, width), lambda b, i, *r: (b * n_tiles + i, x_col(*r)))]


def _gdn_prep_body(prev_ref, init_ref, x_ref, ab_ref, cw_ref, alog_ref, dtb_ref,
                   q_o, k_o, v_o, gb_o, *, tm, t_valid):
    cat = _halo_cat(prev_ref, init_ref, x_ref)
    w = cw_ref[...]
    acc = x_ref[...] * w[GDN_CONV - 1:GDN_CONV, :]
    for s in range(1, GDN_CONV):
        acc = acc + pltpu.roll(cat, s, axis=0)[SUBLANES:] * w[GDN_CONV - 1 - s:GDN_CONV - s, :]
    y = _silu(acc)
    row = pl.program_id(1) * tm + lax.broadcasted_iota(jnp.int32, (tm, 1), 0)
    valid = row < t_valid
    for h in range(N_GDN_HEADS):
        lo = h * HEAD_DIM
        qh = y[:, lo:lo + HEAD_DIM]
        kh = y[:, GDN_W + lo:GDN_W + lo + HEAD_DIM]
        qn = qh * lax.rsqrt(jnp.sum(qh * qh, axis=-1, keepdims=True) + EPS) * (HEAD_DIM ** -0.5)
        kn = kh * lax.rsqrt(jnp.sum(kh * kh, axis=-1, keepdims=True) + EPS)
        q_o[:, lo:lo + HEAD_DIM] = jnp.where(valid, qn, 0.0)
        k_o[:, lo:lo + HEAD_DIM] = jnp.where(valid, kn, 0.0)
    v_o[...] = jnp.where(valid, y[:, 2 * GDN_W:], 0.0)
    ab = ab_ref[...]
    g = -jnp.exp(alog_ref[...]) * _softplus(ab + dtb_ref[...])
    lane = lax.broadcasted_iota(jnp.int32, ab.shape, 1)
    gb = jnp.where(lane < N_GDN_HEADS, g, jnp.where(lane < 2 * N_GDN_HEADS, _sigmoid(ab), 0.0))
    gb_o[...] = jnp.where(valid, gb, 0.0)


def gdn_prep(proj, init, conv_w, a_log, dt_bias, *, n_seq, t_pad, t_valid, tm_pref=256):
    m = proj.shape[0]
    tm = _tile(t_pad, tm_pref)
    n_tiles = t_pad // tm
    w3 = 3 * GDN_W
    pad = lambda a: jnp.pad(a.reshape(1, -1), ((0, 0), (0, LANES - a.shape[-1])))
    ab_blk = COL_AB // LANES
    out = jax.ShapeDtypeStruct((m, GDN_W), f32)
    row_spec = lambda wd: pl.BlockSpec((tm, wd), lambda b, i: (b * n_tiles + i, 0))
    const = lambda r, c: pl.BlockSpec((r, c), lambda b, i: (0, 0))
    return pl.pallas_call(
        functools.partial(_gdn_prep_body, tm=tm, t_valid=t_valid),
        grid=(n_seq, n_tiles),
        in_specs=_halo_specs(n_tiles, tm, SUBLANES, w3, lambda: COL_QKV_B // w3, lambda: 0)
                 + [pl.BlockSpec((tm, LANES), lambda b, i: (b * n_tiles + i, ab_blk)),
                    const(GDN_CONV, w3), const(1, LANES), const(1, LANES)],
        out_specs=[row_spec(GDN_W), row_spec(GDN_W), row_spec(GDN_W), row_spec(LANES)],
        out_shape=[out, out, out, jax.ShapeDtypeStruct((m, LANES), f32)],
        compiler_params=_cparams(("arbitrary", "arbitrary")),
        name="gdn_prep",
    )(proj, init, proj, proj, conv_w, pad(a_log), pad(dt_bias))


def _gdn_intra_body(q_ref, k_ref, v_ref, gb_ref, qe_ref, o0_ref, m_ref, b_ref, gl_ref, *, rt):
    cs = GDN_CHUNK
    nc = rt // cs
    gb = gb_ref[...]
    r = lax.broadcasted_iota(jnp.int32, (rt, rt), 0)
    c = lax.broadcasted_iota(jnp.int32, (rt, rt), 1)
    in_chunk = c >= r - r % cs
    causal_f = jnp.where(c <= r, jnp.where(in_chunk, 1.0, 0.0), 0.0)
    strict_f = jnp.where(c < r, causal_f, 0.0)
    causal, strict = causal_f > 0.0, strict_f > 0.0
    eye = jnp.where(r == c, 1.0, 0.0)
    gcum = jnp.dot(causal_f, gb, precision=HIGHEST, preferred_element_type=f32)
    gl_rows = jnp.concatenate(
        [jnp.broadcast_to(gcum[(j + 1) * cs - 1:(j + 1) * cs, :], (cs, LANES)) for j in range(nc)], axis=0)
    for j in range(nc):
        gl_ref[j] = jnp.exp(gl_rows[j * cs:j * cs + SUBLANES, :])
    heads = range(N_GDN_HEADS)
    cols = [slice(h * HEAD_DIM, (h + 1) * HEAD_DIM) for h in heads]
    q, k, v = ([ref[:, cl] for cl in cols] for ref in (q_ref, k_ref, v_ref))
    gcol = [gcum[:, h:h + 1] for h in heads]
    bcol = [gb[:, N_GDN_HEADS + h:N_GDN_HEADS + h + 1] for h in heads]

    def decay_of(gc):
        gi = jnp.broadcast_to(gc, (rt, rt))
        grow = jnp.sum(jnp.where(r == c, gi, 0.0), axis=0, keepdims=True)
        return jnp.exp(jnp.where(causal, gi - grow, NEG))

    decay = [decay_of(gcol[h]) for h in heads]
    kb = [k[h].astype(bf16) for h in heads]
    a = [jnp.where(strict, bcol[h] * _dot_nt(kb[h], kb[h]) * decay[h], 0.0) for h in heads]
    def in_blocks(size):
        return (r // size) == (c // size)

    a_in = [jnp.where(in_blocks(GDN_BASE), a[h], 0.0) for h in heads]
    x = [eye - a_in[h] for h in heads]
    p = [_dot3(a_in[h], a_in[h]) for h in heads]
    n_sq = max(1, (GDN_BASE - 1).bit_length() - 1)
    for it in range(n_sq):
        x = [x[h] + _dot3(x[h], p[h]) for h in heads]
        if it + 1 < n_sq:
            p = [_dot3(p[h], p[h]) for h in heads]
    size = GDN_BASE
    while size < cs:
        size *= 2
        a_out = [jnp.where(in_blocks(size), a[h], 0.0) - a_in[h] for h in heads]
        x = [x[h] - _dot3(_dot3(x[h], a_out[h]), x[h]) for h in heads]
        a_in = [a_in[h] + a_out[h] for h in heads]
    eg = [jnp.exp(gcol[h]) for h in heads]
    sol = [_dot3(x[h], jnp.concatenate([v[h] * bcol[h], k[h] * (bcol[h] * eg[h])], axis=1))
           for h in heads]
    qk = [jnp.where(causal, _dot_nt(q[h].astype(bf16), kb[h]) * decay[h], 0.0) for h in heads]
    qs = [_bdot(qk[h], sol[h]) for h in heads]
    for h in heads:
        o0_ref[:, cols[h]] = qs[h][:, :HEAD_DIM]
        qe_ref[:, cols[h]] = q[h] * eg[h] - qs[h][:, HEAD_DIM:]
    for h in heads:
        kt = (k[h] * jnp.exp(gl_rows[:, h:h + 1] - gcol[h])).astype(bf16)
        solb = sol[h].astype(bf16)
        for j in range(nc):
            rows = slice(j * cs, (j + 1) * cs)
            mb = _dot_tn(kt[rows], solb[rows])
            b_ref[j, h] = mb[:, :HEAD_DIM]
            m_ref[j, h] = -mb[:, HEAD_DIM:]


def _gdn_scan_body(qe_ref, o0_ref, m_ref, b_ref, gl_ref, z_ref, nw_ref, s0_ref, o_ref, s_out_ref, s_scr,
                   *, cps, n_steps):
    step = pl.program_id(1)
    cs = GDN_CHUNK

    @pl.when(step == 0)
    def _():
        s_scr[...] = s0_ref[...]

    for j in range(cps):
        rows = slice(j * cs, (j + 1) * cs)
        for h in range(N_GDN_HEADS):
            cols = slice(h * HEAD_DIM, (h + 1) * HEAD_DIM)
            s = s_scr[h]
            sb = s.astype(bf16)
            o = jnp.dot(qe_ref[rows, cols].astype(bf16), sb, preferred_element_type=f32) + o0_ref[rows, cols]
            s_scr[h] = (s * gl_ref[j, 0:1, h:h + 1]
                        + jnp.dot(m_ref[j, h].astype(bf16), sb, preferred_element_type=f32) + b_ref[j, h])
            on = o * lax.rsqrt(jnp.mean(o * o, axis=-1, keepdims=True) + EPS) * nw_ref[...]
            o_ref[rows, cols] = (on * _silu(z_ref[rows, cols])).astype(o_ref.dtype)

    @pl.when(step == n_steps - 1)
    def _():
        s_out_ref[...] = s_scr[...]


def gdn_recurrence(q, k, v, gb, z, z_blk, norm_w, s0, s0_layer, *, n_seq, t_pad):
    m = q.shape[0]
    cs = GDN_CHUNK
    assert t_pad % cs == 0
    rt = _tile(m, GDN_TILE)
    assert rt % cs == 0
    nct = rt // cs
    nc_all = m // cs
    h_n = N_GDN_HEADS
    rows = lambda wd: pl.BlockSpec((rt, wd), lambda i: (i, 0))
    mats = pl.BlockSpec((nct, h_n, HEAD_DIM, HEAD_DIM), lambda i: (i, 0, 0, 0))
    mat_shape = jax.ShapeDtypeStruct((nc_all, h_n, HEAD_DIM, HEAD_DIM), f32)
    qe, o0, mm, bb, gl = pl.pallas_call(
        functools.partial(_gdn_intra_body, rt=rt),
        grid=(m // rt,),
        in_specs=[rows(GDN_W), rows(GDN_W), rows(GDN_W), rows(LANES)],
        out_specs=[rows(GDN_W), rows(GDN_W), mats, mats,
                   pl.BlockSpec((nct, SUBLANES, LANES), lambda i: (i, 0, 0))],
        out_shape=[jax.ShapeDtypeStruct((m, GDN_W), f32), jax.ShapeDtypeStruct((m, GDN_W), f32),
                   mat_shape, mat_shape, jax.ShapeDtypeStruct((nc_all, SUBLANES, LANES), f32)],
        compiler_params=_cparams(("arbitrary",)),
        name="gdn_intra",
    )(q, k, v, gb)

    n_chunks = t_pad // cs
    cps = next(c for c in (4, 2, 1) if n_chunks % c == 0)
    n_steps = n_chunks // cps
    srow = lambda wd, cb=0: pl.BlockSpec((cps * cs, wd), lambda b, s: (b * n_steps + s, cb))
    smat = pl.BlockSpec((cps, h_n, HEAD_DIM, HEAD_DIM), lambda b, s: (b * n_steps + s, 0, 0, 0))
    return pl.pallas_call(
        functools.partial(_gdn_scan_body, cps=cps, n_steps=n_steps),
        grid=(n_seq, n_steps),
        in_specs=[srow(GDN_W), srow(GDN_W), smat, smat,
                  pl.BlockSpec((cps, SUBLANES, LANES), lambda b, s: (b * n_steps + s, 0, 0)),
                  srow(GDN_W, z_blk), pl.BlockSpec((1, HEAD_DIM), lambda b, s: (0, 0)),
                  pl.BlockSpec((None, None, h_n, HEAD_DIM, HEAD_DIM), lambda b, s: (s0_layer, b, 0, 0, 0))],
        out_specs=[srow(GDN_W),
                   pl.BlockSpec((None, h_n, HEAD_DIM, HEAD_DIM), lambda b, s: (b, 0, 0, 0))],
        out_shape=[jax.ShapeDtypeStruct((m, GDN_W), bf16),
                   jax.ShapeDtypeStruct((n_seq, h_n, HEAD_DIM, HEAD_DIM), f32)],
        scratch_shapes=[pltpu.VMEM((h_n, HEAD_DIM, HEAD_DIM), f32)],
        compiler_params=_cparams(("arbitrary", "arbitrary")),
        name="gdn_scan",
    )(qe, o0, mm, bb, gl, z, norm_w.reshape(1, HEAD_DIM), s0)


def _pool_body(prev_ref, init_ref, x_ref, pw_ref, ps_ref, o_ref, *, tm, start):
    cat = _halo_cat(prev_ref, init_ref, x_ref)
    sums, s, span = [], cat, 1
    for _ in POOL_WINDOWS:
        s = s + pltpu.roll(s, span, axis=0)
        span *= 2
        sums.append(s)
    assert span == POOL_HALO
    pos = start + pl.program_id(1) * tm + lax.broadcasted_iota(jnp.int32, (tm, 1), 0)
    x = x_ref[...]
    for g, win in enumerate(POOL_WINDOWS):
        lo = g * POOL_GROUP_W
        cnt = jnp.minimum(win, pos + 1).astype(f32)
        mix = sums[g][POOL_HALO:, lo:lo + POOL_GROUP_W] / cnt - x[:, lo:lo + POOL_GROUP_W]
        out = jnp.dot(mix, pw_ref[g], precision=HIGHEST, preferred_element_type=f32)
        o_ref[:, lo:lo + POOL_GROUP_W] = (out * ps_ref[:, lo:lo + POOL_GROUP_W]).astype(o_ref.dtype)


def pool_mix(proj, init, pool_w, pool_scale, *, n_seq, t_pad, start, tm_pref=512):
    m = proj.shape[0]
    tm = _tile(t_pad, tm_pref)
    n_tiles = t_pad // tm
    return pl.pallas_call(
        functools.partial(_pool_body, tm=tm, start=start),
        grid=(n_seq, n_tiles),
        in_specs=_halo_specs(n_tiles, tm, POOL_HALO, POOL_W, lambda: COL_POOL // POOL_W, lambda: 0)
                 + [pl.BlockSpec(pool_w.shape, lambda b, i: (0, 0, 0)),
                    pl.BlockSpec((1, POOL_W), lambda b, i: (0, 0))],
        out_specs=pl.BlockSpec((tm, POOL_W), lambda b, i: (b * n_tiles + i, 0)),
        out_shape=jax.ShapeDtypeStruct((m, POOL_W), _act_dtype(tm)),
        compiler_params=_cparams(("arbitrary", "arbitrary")),
        name="pool_mix",
    )(proj, init, proj, pool_w, pool_scale.reshape(1, POOL_W))


def _conv_act_body(prev_ref, init_ref, a_ref, u_ref, cw_ref, o_ref):
    cat = _halo_cat(prev_ref, init_ref, a_ref)
    w = cw_ref[...]
    acc = a_ref[...] * w[FFN_CONV - 1:FFN_CONV, :]
    for s in range(1, FFN_CONV):
        acc = acc + pltpu.roll(cat, s, axis=0)[SUBLANES:] * w[FFN_CONV - 1 - s:FFN_CONV - s, :]
    o_ref[...] = (_silu(acc) * u_ref[...]).astype(o_ref.dtype)


def conv_act(a, u, init, conv_w, *, n_seq, t_pad, tm_pref=512, tile_elems=512 * 512):
    m, f = a.shape
    tm = _tile(t_pad, tm_pref)
    tf = _tile(f, max(LANES, tile_elems // tm // LANES * LANES))
    n_tiles = t_pad // tm
    return pl.pallas_call(
        _conv_act_body,
        grid=(n_seq, n_tiles, f // tf),
        in_specs=_halo_specs(n_tiles, tm, SUBLANES, tf, lambda j: j, lambda j: j)
                 + [pl.BlockSpec((tm, tf), lambda b, i, j: (b * n_tiles + i, j)),
                    pl.BlockSpec((FFN_CONV, tf), lambda b, i, j: (0, j))],
        out_specs=pl.BlockSpec((tm, tf), lambda b, i, j: (b * n_tiles + i, j)),
        out_shape=jax.ShapeDtypeStruct((m, f), _act_dtype(tm)),
        compiler_params=_cparams(("arbitrary", "arbitrary", "arbitrary")),
        name="conv_act",
    )(a, init, a, u, conv_w)


def _rope_tables(pos):
    half = HEAD_DIM // 2
    inv = ROPE_THETA ** (-jnp.arange(half, dtype=f32) * 2.0 / HEAD_DIM)
    ang = pos.astype(f32)[:, None] * inv[None, :]
    cos, sin = jnp.cos(ang), jnp.sin(ang)
    return jnp.concatenate([cos, cos], axis=-1), jnp.concatenate([-sin, sin], axis=-1)


def _reorder_w_in(w_in):
    qkv = 3 * ATT_W + 3 * GDN_W
    z_lo = qkv
    ab_lo = z_lo + GDN_W
    pool_lo = ab_lo + 2 * N_GDN_HEADS
    assert w_in.shape[-1] == pool_lo + POOL_W
    zeros = jnp.zeros(w_in.shape[:-1] + (COL_Z - COL_AB - 2 * N_GDN_HEADS,), w_in.dtype)
    w = jnp.concatenate([w_in[..., :qkv], w_in[..., pool_lo:], w_in[..., ab_lo:pool_lo], zeros,
                         w_in[..., z_lo:ab_lo]], axis=-1)
    return w.astype(bf16)


def _tail_rows(buf, x, t_valid, lo, hi, n):
    if t_valid >= n:
        return x[:, t_valid - n:t_valid, lo:hi]
    return jnp.concatenate([buf[:, t_valid:], x[:, :t_valid, lo:hi]], axis=1)


def _front_pad(state, rows):
    return jnp.pad(state, ((0, 0), (rows - state.shape[1], 0), (0, 0)))


def _group_layer(x, mods, layer, lw, *, n_seq, t_pad, t_valid, start, states, attn_fn):
    (g_mix, g_ffn, w_in_r, gdn_conv_w, gdn_a_log, gdn_dt_bias, gdn_norm_w, pool_w, pool_scale,
     w_out, w_gate, w_up, ffn_conv_w, w_down) = lw
    sh1, sc1, gt1, sh2, sc2, gt2 = mods
    s_gdn, s_gdn_layer, gdn_buf, pool_buf, ffn_buf = states
    m = x.shape[0]
    (proj,) = matmul(norm_mod(x, g_mix, sc1, sh1), [w_in_r], layer, name="in_proj")
    proj3 = proj.reshape(n_seq, t_pad, IN_W_PAD)

    o_a, k_new, v_new = attn_fn(proj)

    qn, kn, vc, gb = gdn_prep(proj, _front_pad(gdn_buf, SUBLANES), gdn_conv_w, gdn_a_log, gdn_dt_bias,
                              n_seq=n_seq, t_pad=t_pad, t_valid=t_valid)
    t_gdn = -(-t_pad // GDN_CHUNK) * GDN_CHUNK
    if t_gdn == t_pad:
        o_b, s_new = gdn_recurrence(qn, kn, vc, gb, proj, COL_Z // GDN_W, gdn_norm_w, s_gdn, s_gdn_layer,
                                    n_seq=n_seq, t_pad=t_pad)
    else:
        def chunk_pad(a):
            a = a.reshape(n_seq, t_pad, a.shape[-1])
            return jnp.pad(a, ((0, 0), (0, t_gdn - t_pad), (0, 0))).reshape(n_seq * t_gdn, -1)
        o_b, s_new = gdn_recurrence(chunk_pad(qn), chunk_pad(kn), chunk_pad(vc), chunk_pad(gb),
                                    chunk_pad(proj[:, COL_Z:COL_Z + GDN_W]), 0, gdn_norm_w, s_gdn,
                                    s_gdn_layer, n_seq=n_seq, t_pad=t_gdn)
        o_b = o_b.reshape(n_seq, t_gdn, GDN_W)[:, :t_pad].reshape(m, GDN_W)
    new_gdn_buf = _tail_rows(gdn_buf, proj3, t_valid, COL_QKV_B, COL_QKV_B + 3 * GDN_W, GDN_CONV - 1)

    o_p = pool_mix(proj, _front_pad(pool_buf, POOL_HALO), pool_w, pool_scale,
                   n_seq=n_seq, t_pad=t_pad, start=start)
    new_pool_buf = _tail_rows(pool_buf, proj3, t_valid, COL_POOL, COL_POOL + POOL_W, POOL_BUF)

    (x,) = matmul([o_a, o_b, o_p], [w_out], layer, x=x, gt=gt1, name="out_proj")

    h2 = norm_mod(x, g_ffn, sc2, sh2)
    init_a = _front_pad(ffn_buf, SUBLANES)
    if n_seq == 1 and t_valid == t_pad and t_pad >= SUBLANES:
        act, a_tail = ffn_up_conv(h2, w_gate, w_up, layer, init_a[0], ffn_conv_w)
        new_ffn_buf = a_tail[None, SUBLANES - (FFN_CONV - 1):]
    else:
        a, u = matmul(h2, [w_gate, w_up], layer, name="ffn_up")
        act = conv_act(a, u, init_a, ffn_conv_w, n_seq=n_seq, t_pad=t_pad)
        f = a.shape[1]
        new_ffn_buf = _tail_rows(ffn_buf, a.reshape(n_seq, t_pad, f), t_valid, 0, f, FFN_CONV - 1)
    (x,) = matmul(act, [w_down], layer, x=x, gt=gt2, tm_pref=512, name="ffn_down")
    return x, k_new, v_new, s_new, new_gdn_buf, new_pool_buf, new_ffn_buf


def kernel(x_prompt, x_sample, cache_k, cache_v, page_table, state_gdn, state_gdn_conv, state_pool,
           state_ffn_conv, c_prompt, c_sample, w_ada, b_ada, g_mix, g_ffn, w_in, gdn_conv_w, gdn_a_log,
           gdn_dt_bias, gdn_norm_w, pool_w, pool_scale, w_out, w_gate, w_up, ffn_conv_w, w_down, g_final):
    bp, t_p, d = x_prompt.shape
    bs, t_s, _ = x_sample.shape
    depth = w_ada.shape[0]
    page = cache_k.shape[2]
    past_len = page_table.shape[1] * page
    assert bp == 1 and t_s <= SEQ_PAD and t_p % MOBA_BLOCK == 0 and past_len % MOBA_BLOCK == 0
    assert t_p >= POOL_BUF

    ck = jnp.transpose(cache_k, (0, 1, 3, 2, 4))
    cv = jnp.transpose(cache_v, (0, 1, 3, 2, 4))

    n_c = bp + bs
    mc = -(-n_c // SUBLANES) * SUBLANES
    c_all = jnp.pad(jnp.concatenate([c_prompt, c_sample], axis=0), ((0, mc - n_c), (0, 0)))
    mod = ada_mod(c_all, w_ada, b_ada)
    w_in_r = _reorder_w_in(w_in)

    cos_p, sin_p = _rope_tables(jnp.arange(t_p))
    cos_s, sin_s = _rope_tables(past_len + jnp.arange(SEQ_PAD))

    xp = x_prompt.reshape(bp * t_p, d)
    xs = jnp.pad(x_sample, ((0, 0), (0, SEQ_PAD - t_s), (0, 0))).reshape(bs * SEQ_PAD, d)

    zeros_p = lambda *shape: jnp.zeros((bp,) + shape, f32)
    outs_p, outs_s = [], []
    for l in range(depth):
        lw = (g_mix[l], g_ffn[l], w_in_r, gdn_conv_w[l], gdn_a_log[l], gdn_dt_bias[l], gdn_norm_w[l],
              pool_w[l], pool_scale[l], w_out, w_gate, w_up, ffn_conv_w[l], w_down)
        mods_p = [mod[l, 0:bp, i * d:(i + 1) * d] for i in range(6)]
        mods_s = [jnp.repeat(mod[l, bp:n_c, i * d:(i + 1) * d], SEQ_PAD, axis=0) for i in range(6)]

        def attn_p(proj):
            qt, k, v, kb, vt, km = attn_prep_prompt_t(proj, cos_p, sin_p)
            return attn_prompt_t(qt, kb, vt, km), k, v

        def attn_s(proj, l=l):
            q, k, v = attn_prep_sample(proj, cos_s, sin_s, bs)
            sel = sample_select(ck, l, page_table, q)
            sel = sel[:, :t_s, :N_ATT_HEADS * MOBA_TOPK].reshape(bs, -1)
            o = sample_attention(ck, cv, l, page_table, sel, q, k, v, t_s)
            return o.reshape(bs * SEQ_PAD, ATT_W), k[:, :, :t_s], v[:, :, :t_s]

        xp, *st_p = _group_layer(
            xp, mods_p, l, lw, n_seq=bp, t_pad=t_p, t_valid=t_p, start=0, attn_fn=attn_p,
            states=(zeros_p(N_GDN_HEADS, HEAD_DIM, HEAD_DIM)[None], 0, zeros_p(GDN_CONV - 1, 3 * GDN_W),
                    zeros_p(POOL_BUF, POOL_W), zeros_p(FFN_CONV - 1, w_gate.shape[-1])))
        xs, *st_s = _group_layer(
            xs, mods_s, l, lw, n_seq=bs, t_pad=SEQ_PAD, t_valid=t_s, start=past_len, attn_fn=attn_s,
            states=(state_gdn, l, state_gdn_conv[l], state_pool[l], state_ffn_conv[l]))
        outs_p.append(st_p)
        outs_s.append(st_s)

    y_prompt = rms_norm(xp, g_final).reshape(bp, t_p, d)
    y_sample = rms_norm(xs, g_final).reshape(bs, SEQ_PAD, d)[:, :t_s]
    kp, vp, gp, gcp, pp, fp = [jnp.stack(t) for t in zip(*outs_p)]
    ks, vs, gs, gcs, ps, fs = [jnp.stack(t) for t in zip(*outs_s)]
    k_prompt = jnp.transpose(kp, (0, 2, 1, 3))[:, None]
    v_prompt = jnp.transpose(vp, (0, 2, 1, 3))[:, None]
    k_sample = jnp.transpose(ks, (0, 1, 3, 2, 4))
    v_sample = jnp.transpose(vs, (0, 1, 3, 2, 4))
    return (y_prompt, y_sample, k_prompt, v_prompt, k_sample, v_sample, gp, gs, gcp, gcs, pp, ps, fp, fs)
```
